```python
import math
import jax, jax.numpy as jnp
from jax import lax
import numpy as np

D_MODEL = 1024
BATCH = 16
SEQ = 256
DEPTH = 4
DEC_BATCH = 2
DEC_SEQ = 4096
PAST_LEN = 256

GRID_W = 64
N_MIXERS = 3
N_FNET = (DEPTH + 2) // 3
N_RET = (DEPTH + 1) // 3
N_ATTN = DEPTH // 3

FNET_GROUPS = 4
FNET_GROUP_DIM = D_MODEL // FNET_GROUPS

RET_HEADS = 4
RET_DK = D_MODEL // RET_HEADS
RET_DV = 2 * RET_DK
RET_NQ = RET_HEADS * RET_DK
RET_NV = RET_HEADS * RET_DV
RET_IN = 2 * RET_NQ + 2 * RET_NV
RET_CHUNK = 128

ATTN_HEADS = 16
ATTN_KV_HEADS = 4
ATTN_REP = ATTN_HEADS // ATTN_KV_HEADS
HEAD_DIM = D_MODEL // ATTN_HEADS
AXIS_DIM = HEAD_DIM // 2
ATTN_IN = (ATTN_HEADS + 2 * ATTN_KV_HEADS) * HEAD_DIM
WINDOW = 128
ATTN_BLOCK = 128
ROPE_BASE = 10000.0

MOE_GROUPS = 4
MOE_EXPERTS_PER_GROUP = 8
MOE_EXPERTS = MOE_GROUPS * MOE_EXPERTS_PER_GROUP
MOE_TOP_K = 2
MOE_D_FF = 512
MOE_BLOCK = 128

EPS = 1e-6
NEG_INF = -1e30

kernel_name = "hybrid_fnet_retnet_swa_hmoe_diffusion_step"


def rms_norm(x, g):
    xf = x.astype(jnp.float32)
    y = xf * lax.rsqrt(jnp.mean(xf * xf, axis=-1, keepdims=True) + EPS)
    return (y * g.astype(jnp.float32)).astype(x.dtype)


def adaln(cond, w, b):
    m = jax.nn.silu(cond) @ w + b
    return jnp.split(m, 6, axis=-1)


def modulate(h, shift, scale):
    return h * (1 + scale[:, None, :]) + shift[:, None, :]


def fourier_mix(h, w):
    b_, l_, _ = h.shape
    hf = h.astype(jnp.float32).reshape(b_, l_, FNET_GROUPS, FNET_GROUP_DIM)
    mixed = jnp.fft.fftn(hf, axes=(1, 3), norm="ortho").real
    return mixed.reshape(b_, l_, D_MODEL).astype(h.dtype) @ w


def retention_scan(q, k, v, log_gamma, s0, strict):
    b_, l_ = q.shape[:2]
    n = l_ // RET_CHUNK

    def to_chunks(t):
        return t.reshape(b_, n, RET_CHUNK, RET_HEADS, t.shape[-1]).transpose(1, 0, 3, 2, 4)

    idx = jnp.arange(RET_CHUNK, dtype=jnp.float32)
    diff = idx[:, None] - idx[None, :]
    mask = (diff > 0) if strict else (diff >= 0)
    dmat = jnp.where(mask[None], jnp.exp(jnp.maximum(diff, 0.0)[None] * log_gamma[:, None, None]), 0.0)
    xi = jnp.exp((idx[None, :] + 1.0) * log_gamma[:, None])
    zeta = jnp.exp((RET_CHUNK - 1.0 - idx[None, :]) * log_gamma[:, None])
    chunk_decay = jnp.exp(RET_CHUNK * log_gamma)

    def step(s, inp):
        qb, kb, vb = inp
        inner = jnp.einsum('bhnd,bhmd->bhnm', qb, kb) * dmat
        o = (jnp.einsum('bhnm,bhme->bhne', inner, vb)
             + jnp.einsum('bhnd,bhde->bhne', qb * xi[..., None], s))
        s = s * chunk_decay[:, None, None] + jnp.einsum('bhmd,bhme->bhde', kb * zeta[..., None], vb)
        return s, o

    s_fin, o = lax.scan(step, s0, (to_chunks(q), to_chunks(k), to_chunks(v)))
    o = o.transpose(1, 0, 3, 2, 4).reshape(b_, l_, RET_HEADS, RET_DV)
    return o, s_fin


def retention_mix(h, w_in, w_out, dec_fwd, dec_bwd, s0_fwd, s0_bwd):
    b_, l_, _ = h.shape
    proj = (h @ w_in).astype(jnp.float32)
    q, k, v, g = jnp.split(proj, [RET_NQ, 2 * RET_NQ, 2 * RET_NQ + RET_NV], axis=-1)
    q = q.reshape(b_, l_, RET_HEADS, RET_DK)
    k = k.reshape(b_, l_, RET_HEADS, RET_DK) * (RET_DK ** -0.5)
    v = v.reshape(b_, l_, RET_HEADS, RET_DV)
    lg_f = jax.nn.log_sigmoid(dec_fwd.astype(jnp.float32))
    lg_b = jax.nn.log_sigmoid(dec_bwd.astype(jnp.float32))
    o_f, s_f = retention_scan(q, k, v, lg_f, s0_fwd.astype(jnp.float32), False)
    o_b, s_b = retention_scan(q[:, ::-1], k[:, ::-1], v[:, ::-1], lg_b, s0_bwd.astype(jnp.float32), True)
    o = o_f + o_b[:, ::-1]
    mu = jnp.mean(o, axis=-1, keepdims=True)
    var = jnp.mean(jnp.square(o - mu), axis=-1, keepdims=True)
    o = (o - mu) * lax.rsqrt(var + EPS)
    y = jax.nn.silu(g) * o.reshape(b_, l_, RET_NV)
    return y.astype(h.dtype) @ w_out, s_f, s_b


def attn_project(h, w_in):
    b_, l_, _ = h.shape
    proj = h @ w_in
    nq = ATTN_HEADS * HEAD_DIM
    nk = ATTN_KV_HEADS * HEAD_DIM
    q = proj[..., :nq].reshape(b_, l_, ATTN_HEADS, HEAD_DIM)
    k = proj[..., nq:nq + nk].reshape(b_, l_, ATTN_KV_HEADS, HEAD_DIM)
    v = proj[..., nq + nk:].reshape(b_, l_, ATTN_KV_HEADS, HEAD_DIM)
    return q, k, v


def rotate_axis(x, ang):
    c = jnp.cos(ang)[None, :, None, :]
    s = jnp.sin(ang)[None, :, None, :]
    x1, x2 = x[..., :AXIS_DIM // 2], x[..., AXIS_DIM // 2:]
    return jnp.concatenate([x1 * c - x2 * s, x2 * c + x1 * s], axis=-1)


def axial_rope(x, ang_row, ang_col):
    return jnp.concatenate([rotate_axis(x[..., :AXIS_DIM], ang_row),
                            rotate_axis(x[..., AXIS_DIM:], ang_col)], axis=-1)


def sink_logits(sink, b_):
    s = sink.astype(jnp.float32).reshape(ATTN_KV_HEADS, ATTN_REP)
    return jnp.broadcast_to(s[None, :, :, None, None], (b_, ATTN_KV_HEADS, ATTN_REP, ATTN_BLOCK, 1))


def context_attention(q, k, v, sink):
    b_, s_ = q.shape[:2]
    n = s_ // ATTN_BLOCK
    scale = HEAD_DIM ** -0.5
    kf = k.astype(jnp.float32)
    vf = v.astype(jnp.float32)
    qb = q.astype(jnp.float32).reshape(b_, n, ATTN_BLOCK, ATTN_KV_HEADS, ATTN_REP, HEAD_DIM)
    qb = qb.transpose(1, 0, 2, 3, 4, 5)
    snk = sink_logits(sink, b_)

    def blk(qi):
        s = jnp.einsum('bqgrd,bkgd->bgrqk', qi, kf) * scale
        p = jax.nn.softmax(jnp.concatenate([s, snk], axis=-1), axis=-1)
        return jnp.einsum('bgrqk,bkgd->bqgrd', p[..., :s_], vf)

    o = lax.map(blk, qb)
    return o.transpose(1, 0, 2, 3, 4, 5).reshape(b_, s_, ATTN_HEADS * HEAD_DIM).astype(q.dtype)


def latent_attention(q, k, v, kc, vc, sink):
    b_, l_ = q.shape[:2]
    p_ = kc.shape[1]
    n = l_ // ATTN_BLOCK
    scale = HEAD_DIM ** -0.5
    band = 3 * ATTN_BLOCK
    pad = ((0, 0), (ATTN_BLOCK, ATTN_BLOCK), (0, 0), (0, 0))

    def banded(t):
        tp = jnp.pad(t.astype(jnp.float32), pad).reshape(b_, n + 2, ATTN_BLOCK, ATTN_KV_HEADS, HEAD_DIM)
        tb = jnp.concatenate([tp[:, :-2], tp[:, 1:-1], tp[:, 2:]], axis=2)
        return tb.transpose(1, 0, 2, 3, 4)

    kband = banded(k)
    vband = banded(v)
    qb = q.astype(jnp.float32).reshape(b_, n, ATTN_BLOCK, ATTN_KV_HEADS, ATTN_REP, HEAD_DIM)
    qb = qb.transpose(1, 0, 2, 3, 4, 5)
    kcf = kc.astype(jnp.float32)
    vcf = vc.astype(jnp.float32)
    snk = sink_logits(sink, b_)
    qa = jnp.arange(ATTN_BLOCK)
    kb_idx = jnp.arange(band)
    in_window = jnp.abs((kb_idx[None, :] - ATTN_BLOCK) - qa[:, None]) <= WINDOW

    def blk(inp):
        j, qi, kb, vb = inp
        kpos = j * ATTN_BLOCK - ATTN_BLOCK + kb_idx
        valid = in_window & ((kpos >= 0) & (kpos < l_))[None, :]
        s_loc = jnp.where(valid, jnp.einsum('bqgrd,bkgd->bgrqk', qi, kb) * scale, NEG_INF)
        s_ctx = jnp.einsum('bqgrd,bkgd->bgrqk', qi, kcf) * scale
        p = jax.nn.softmax(jnp.concatenate([s_loc, s_ctx, snk], axis=-1), axis=-1)
        return (jnp.einsum('bgrqk,bkgd->bqgrd', p[..., :band], vb)
                + jnp.einsum('bgrqk,bkgd->bqgrd', p[..., band:band + p_], vcf))

    o = lax.map(blk, (jnp.arange(n), qb, kband, vband))
    return o.transpose(1, 0, 2, 3, 4, 5).reshape(b_, l_, ATTN_HEADS * HEAD_DIM)


def hier_route(xt, wg, bg, we, be):
    xf = xt.astype(jnp.float32)
    lg = xf @ wg.astype(jnp.float32) + bg.astype(jnp.float32)
    pg = jax.nn.softmax(lg, axis=-1)
    g_top = jnp.argmax(lg, axis=-1)
    pg_sel = jnp.take_along_axis(pg, g_top[:, None], axis=1)[:, 0]
    le = jnp.einsum('td,gde->tge', xf, we.astype(jnp.float32)) + be.astype(jnp.float32)
    le_sel = jnp.take_along_axis(le, g_top[:, None, None], axis=1)[:, 0]
    top_v, top_i = lax.top_k(le_sel, MOE_TOP_K)
    gates = pg_sel[:, None] * jax.nn.softmax(top_v, axis=-1)
    experts = g_top[:, None] * MOE_EXPERTS_PER_GROUP + top_i
    return experts, gates


def moe_ffn(x, wg, bg, we, be, w1, w3, w2):
    b_, l_, d_ = x.shape
    t_ = b_ * l_
    xt = x.reshape(t_, d_)
    experts, gates = hier_route(xt, wg, bg, we, be)
    a_ = t_ * MOE_TOP_K
    e = experts.reshape(a_)
    src = jnp.arange(a_) // MOE_TOP_K
    gt = gates.reshape(a_)
    order = jnp.argsort(e)
    e_s, src_s, g_s = e[order], src[order], gt[order]
    counts = jax.ops.segment_sum(jnp.ones((a_,), jnp.int32), e, num_segments=MOE_EXPERTS)
    padded = (counts + MOE_BLOCK - 1) // MOE_BLOCK * MOE_BLOCK
    pad_end = jnp.cumsum(padded)
    pad_start = pad_end - padded
    start = jnp.cumsum(counts) - counts
    dest = pad_start[e_s] + jnp.arange(a_) - start[e_s]
    n_blocks = -(-a_ // MOE_BLOCK) + MOE_EXPERTS
    buf = jnp.zeros((n_blocks * MOE_BLOCK, d_), x.dtype).at[dest].set(xt[src_s])
    block_expert = jnp.minimum(
        jnp.searchsorted(pad_end, jnp.arange(n_blocks) * MOE_BLOCK, side='right'), MOE_EXPERTS - 1)

    def run(inp):
        xb, eidx = inp
        hmid = jax.nn.silu(xb @ w1[eidx]) * (xb @ w3[eidx])
        return hmid @ w2[eidx]

    out = lax.map(run, (buf.reshape(n_blocks, MOE_BLOCK, d_), block_expert)).reshape(n_blocks * MOE_BLOCK, d_)
    y = jnp.zeros((t_, d_), jnp.float32).at[src_s].add(out[dest].astype(jnp.float32) * g_s[:, None])
    return y.astype(x.dtype).reshape(b_, l_, d_)


def setup_inputs(seed: int = 0) -> dict:
    key = jax.random.key(seed)
    ks = jax.random.split(key, 32)
    f32 = jnp.float32
    nrm = lambda k, shape, s: jax.random.normal(k, shape, f32) * s
    gamma0 = 1.0 - 2.0 ** (-5.0 - jnp.arange(RET_HEADS, dtype=f32))
    dec_logit = jnp.log(gamma0 / (1.0 - gamma0))
    return {
        "x_prompt": nrm(ks[0], (BATCH, SEQ, D_MODEL), 1.0),
        "x_sample": nrm(ks[1], (DEC_BATCH, DEC_SEQ, D_MODEL), 1.0),
        "c": nrm(ks[2], (DEC_BATCH, D_MODEL), 1.0),
        "cache_attn_k": nrm(ks[3], (DEC_BATCH, N_ATTN, PAST_LEN, ATTN_KV_HEADS, HEAD_DIM), 1.0),
        "cache_attn_v": nrm(ks[4], (DEC_BATCH, N_ATTN, PAST_LEN, ATTN_KV_HEADS, HEAD_DIM), 1.0),
        "state_ret_fwd": nrm(ks[5], (DEC_BATCH, N_RET, RET_HEADS, RET_DK, RET_DV), 0.5),
        "state_ret_bwd": nrm(ks[6], (DEC_BATCH, N_RET, RET_HEADS, RET_DK, RET_DV), 0.5),
        "c_ctx": nrm(ks[7], (D_MODEL,), 1.0),
        "adaln_w": nrm(ks[8], (DEPTH, D_MODEL, 6 * D_MODEL), 0.5 * D_MODEL ** -0.5),
        "adaln_b": nrm(ks[9], (DEPTH, 6 * D_MODEL), 0.02),
        "norm1_g": 1.0 + nrm(ks[10], (DEPTH, D_MODEL), 0.05),
        "norm2_g": 1.0 + nrm(ks[11], (DEPTH, D_MODEL), 0.05),
        "final_g": 1.0 + nrm(ks[12], (D_MODEL,), 0.05),
        "fnet_w": nrm(ks[13], (N_FNET, D_MODEL, D_MODEL), D_MODEL ** -0.5),
        "ret_w_in": nrm(ks[14], (N_RET, D_MODEL, RET_IN), D_MODEL ** -0.5),
        "ret_w_out": nrm(ks[15], (N_RET, RET_NV, D_MODEL), RET_NV ** -0.5),
        "ret_decay_fwd": dec_logit[None, :] + nrm(ks[16], (N_RET, RET_HEADS), 0.1),
        "ret_decay_bwd": dec_logit[None, :] + nrm(ks[17], (N_RET, RET_HEADS), 0.1),
        "attn_w_in": nrm(ks[18], (N_ATTN, D_MODEL, ATTN_IN), D_MODEL ** -0.5),
        "attn_w_out": nrm(ks[19], (N_ATTN, ATTN_HEADS * HEAD_DIM, D_MODEL), (ATTN_HEADS * HEAD_DIM) ** -0.5),
        "attn_sink": nrm(ks[20], (N_ATTN, ATTN_HEADS), 0.5),
        "router_group_w": nrm(ks[21], (DEPTH, D_MODEL, MOE_GROUPS), D_MODEL ** -0.5),
        "router_group_b": nrm(ks[22], (DEPTH, MOE_GROUPS), 0.01),
        "router_expert_w": nrm(ks[23], (DEPTH, MOE_GROUPS, D_MODEL, MOE_EXPERTS_PER_GROUP), D_MODEL ** -0.5),
        "router_expert_b": nrm(ks[24], (DEPTH, MOE_GROUPS, MOE_EXPERTS_PER_GROUP), 0.01),
        "moe_w1": nrm(ks[25], (DEPTH, MOE_EXPERTS, D_MODEL, MOE_D_FF), D_MODEL ** -0.5),
        "moe_w3": nrm(ks[26], (DEPTH, MOE_EXPERTS, D_MODEL, MOE_D_FF), D_MODEL ** -0.5),
        "moe_w2": nrm(ks[27], (DEPTH, MOE_EXPERTS, MOE_D_FF, D_MODEL), MOE_D_FF ** -0.5),
    }


def reference(x_prompt, x_sample, c, cache_attn_k, cache_attn_v, state_ret_fwd, state_ret_bwd,
              c_ctx, adaln_w, adaln_b, norm1_g, norm2_g, final_g, fnet_w,
              ret_w_in, ret_w_out, ret_decay_fwd, ret_decay_bwd,
              attn_w_in, attn_w_out, attn_sink,
              router_group_w, router_group_b, router_expert_w, router_expert_b,
              moe_w1, moe_w3, moe_w2):
    xp = x_prompt
    bp = xp.shape[0]
    ctx_k, ctx_v, ctx_sf, ctx_sb = [], [], [], []
    for i in range(DEPTH):
        sh1, sc1, g1, sh2, sc2, g2 = adaln(c_ctx[None, :], adaln_w[i], adaln_b[i])
        h = modulate(rms_norm(xp, norm1_g[i]), sh1, sc1)
        kind, j = i % N_MIXERS, i // N_MIXERS
        if kind == 0:
            o = fourier_mix(h, fnet_w[j])
        elif kind == 1:
            s0 = jnp.zeros((bp, RET_HEADS, RET_DK, RET_DV), jnp.float32)
            o, sf, sb = retention_mix(h, ret_w_in[j], ret_w_out[j], ret_decay_fwd[j], ret_decay_bwd[j], s0, s0)
            ctx_sf.append(sf.astype(xp.dtype))
            ctx_sb.append(sb.astype(xp.dtype))
        else:
            q, k, v = attn_project(h, attn_w_in[j])
            o = context_attention(q, k, v, attn_sink[j]) @ attn_w_out[j]
            ctx_k.append(k)
            ctx_v.append(v)
        xp = xp + g1[:, None, :] * o
        h = modulate(rms_norm(xp, norm2_g[i]), sh2, sc2)
        xp = xp + g2[:, None, :] * moe_ffn(h, router_group_w[i], router_group_b[i], router_expert_w[i],
                                            router_expert_b[i], moe_w1[i], moe_w3[i], moe_w2[i])
    y_prompt = rms_norm(xp, final_g)
    new_cache_attn_k = jnp.stack(ctx_k, axis=1)
    new_cache_attn_v = jnp.stack(ctx_v, axis=1)
    new_state_ret_fwd = jnp.stack(ctx_sf, axis=1)
    new_state_ret_bwd = jnp.stack(ctx_sb, axis=1)

    xs = x_sample
    l_lat = xs.shape[1]
    rows = l_lat // GRID_W
    row_idx = jnp.repeat(jnp.arange(rows), GRID_W).astype(jnp.float32)
    col_idx = jnp.tile(jnp.arange(GRID_W), rows).astype(jnp.float32)
    inv_freq = ROPE_BASE ** (-jnp.arange(0, AXIS_DIM, 2, dtype=jnp.float32) / AXIS_DIM)
    ang_row = row_idx[:, None] * inv_freq[None, :]
    ang_col = col_idx[:, None] * inv_freq[None, :]
    for i in range(DEPTH):
        sh1, sc1, g1, sh2, sc2, g2 = adaln(c, adaln_w[i], adaln_b[i])
        h = modulate(rms_norm(xs, norm1_g[i]), sh1, sc1)
        kind, j = i % N_MIXERS, i // N_MIXERS
        if kind == 0:
            o = fourier_mix(h, fnet_w[j])
        elif kind == 1:
            o, _, _ = retention_mix(h, ret_w_in[j], ret_w_out[j], ret_decay_fwd[j], ret_decay_bwd[j],
                                    state_ret_fwd[:, j], state_ret_bwd[:, j])
        else:
            q, k, v = attn_project(h, attn_w_in[j])
            q = axial_rope(q.astype(jnp.float32), ang_row, ang_col)
            k = axial_rope(k.astype(jnp.float32), ang_row, ang_col)
            o = latent_attention(q, k, v, cache_attn_k[:, j], cache_attn_v[:, j], attn_sink[j])
            o = o.astype(h.dtype) @ attn_w_out[j]
        xs = xs + g1[:, None, :] * o
        h = modulate(rms_norm(xs, norm2_g[i]), sh2, sc2)
        xs = xs + g2[:, None, :] * moe_ffn(h, router_group_w[i], router_group_b[i], router_expert_w[i],
                                            router_expert_b[i], moe_w1[i], moe_w3[i], moe_w2[i])
    y_sample = rms_norm(xs, final_g)
    return (y_prompt, y_sample, new_cache_attn_k, new_cache_attn_v, new_state_ret_fwd, new_state_ret_bwd)
```

```python
import functools
import math

import jax
import jax.numpy as jnp
from jax import lax
from jax.experimental import pallas as pl
from jax.experimental.pallas import tpu as pltpu

F32 = jnp.float32
BF16 = jnp.bfloat16
I32 = jnp.int32

D = 1024
DEPTH = 4
SEG = 4096
FNET_GROUP_DIM = 256
RET_HEADS, RET_DK, RET_DV = 4, 256, 512
RET_C = 256
ATTN_HEADS, ATTN_KV, HEAD_DIM, ATTN_REP = 16, 4, 64, 4
WINDOW, QBLK = 128, 128
GRID_W = 64
ROPE_BASE = 10000.0
N_EXP, EXP_PER_GROUP, N_GROUPS, D_FF = 32, 8, 4, 512
EPS = 1e-6
NEG_INF = -1e30
LANE = 128
SUB = 8

V7X_VMEM_LIMIT = 56 * 1024 * 1024
MOE_PAD = 64
MOE_TILE_STRIDE = 264


def _cparams(n_axes):
    return pltpu.CompilerParams(dimension_semantics=("arbitrary",) * n_axes, vmem_limit_bytes=V7X_VMEM_LIMIT)


def _dot(a, b):
    return jnp.dot(a, b, preferred_element_type=F32)


def _dot_nt(a, b):
    return lax.dot_general(a, b, (((1,), (1,)), ((), ())), preferred_element_type=F32)


def _split(x):
    hi = x.astype(BF16)
    lo = (x - hi.astype(F32)).astype(BF16)
    return hi, lo


def _dot3(a, b):
    ah, al = _split(a)
    bh, bl = _split(b)
    return _dot(ah, bh) + (_dot(ah, bl) + _dot(al, bh))


def _silu(x):
    return x * jax.nn.sigmoid(x)


def _norm_mod(x, g, sh, sc):
    ms = jnp.mean(x * x, axis=-1, keepdims=True)
    return (x * lax.rsqrt(ms + EPS) * g) * (1.0 + sc) + sh


def _seg_of(is_lat, tm):
    if is_lat:
        return lambda i: 1 + (i * tm) // SEG
    return lambda i: 0


def _mod_spec(layer, kind, seg_fn):
    return pl.BlockSpec((None, None, None, 1, D), lambda i, *_: (layer, seg_fn(i), kind, 0, 0))


def _adaln_kernel(c_ref, w_ref, b_ref, o_ref):
    o_ref[...] = _dot3(_silu(c_ref[...]), w_ref[...]) + b_ref[...]


def _adaln(cond8, w, b):
    depth, _, n = w.shape
    tn = 1536
    return pl.pallas_call(
        _adaln_kernel,
        grid=(depth, n // tn),
        in_specs=[pl.BlockSpec((SUB, D), lambda l, j: (0, 0)),
                  pl.BlockSpec((None, D, tn), lambda l, j: (l, 0, j)),
                  pl.BlockSpec((None, 1, tn), lambda l, j: (l, 0, j))],
        out_specs=pl.BlockSpec((None, SUB, tn), lambda l, j: (l, 0, j)),
        out_shape=jax.ShapeDtypeStruct((depth, SUB, n), F32),
        compiler_params=_cparams(2),
        name="adaln",
    )(cond8, w, b.reshape(depth, 1, n))


def _rope(x, cos_t, sin_t):
    w = x.shape[1]
    reps = w // LANE
    c = jnp.concatenate([cos_t] * reps, axis=1)
    s = jnp.concatenate([sin_t] * reps, axis=1)
    lane = lax.broadcasted_iota(I32, x.shape, 1)
    first = (lane % 32) < 16
    partner = jnp.where(first, pltpu.roll(x, w - 16, 1), pltpu.roll(x, 16, 1))
    return x * c + partner * s


def _proj_kernel(x_ref, g_ref, sh_ref, sc_ref, w_ref, *rest, n_rope):
    if n_rope:
        cos_ref, sin_ref, o_ref = rest
    else:
        (o_ref,) = rest
    hb = _norm_mod(x_ref[...], g_ref[...], sh_ref[...], sc_ref[...]).astype(BF16)
    n = w_ref.shape[1]
    tn = 512
    if n_rope:
        q_w = ATTN_HEADS * HEAD_DIM
        k_w = ATTN_KV * HEAD_DIM
        q = _dot(hb, w_ref[:, :q_w])
        o_ref[:, :q_w] = _rope(q, cos_ref[...], sin_ref[...])
        k = _dot(hb, w_ref[:, q_w:q_w + k_w])
        o_ref[:, q_w:q_w + k_w] = _rope(k, cos_ref[...], sin_ref[...])
        o_ref[:, q_w + k_w:] = _dot(hb, w_ref[:, q_w + k_w:])
    else:
        for j in range(n // tn):
            o_ref[:, j * tn:(j + 1) * tn] = _dot(hb, w_ref[:, j * tn:(j + 1) * tn]).astype(o_ref.dtype)


def _proj(x, mods, layer, sub, norm_g, w_bf16, is_lat, out_dtype, rope=None):
    t = x.shape[0]
    n = w_bf16.shape[1]
    tm = 256
    seg_fn = _seg_of(is_lat, tm)
    in_specs = [pl.BlockSpec((tm, D), lambda i: (i, 0)),
                pl.BlockSpec((1, D), lambda i: (0, 0)),
                _mod_spec(layer, 3 * sub + 0, seg_fn),
                _mod_spec(layer, 3 * sub + 1, seg_fn),
                pl.BlockSpec((D, n), lambda i: (0, 0))]
    args = [x, norm_g.reshape(1, D), mods, mods, w_bf16]
    if rope is not None:
        in_specs += [pl.BlockSpec((tm, LANE), lambda i: (i, 0))] * 2
        args += list(rope)
    return pl.pallas_call(
        functools.partial(_proj_kernel, n_rope=rope is not None),
        grid=(t // tm,),
        in_specs=in_specs,
        out_specs=pl.BlockSpec((tm, n), lambda i: (i, 0)),
        out_shape=jax.ShapeDtypeStruct((t, n), out_dtype),
        compiler_params=_cparams(1),
        name="norm_mod_proj",
    )(*args)


def _linres_kernel(y_ref, w_ref, x_ref, g_ref, o_ref):
    o_ref[...] = x_ref[...] + g_ref[...] * _dot(y_ref[...].astype(BF16), w_ref[...])


def _linres(y, w_bf16, x, mods, layer, kind, is_lat):
    t, k = y.shape
    tm = 256
    seg_fn = _seg_of(is_lat, tm)
    return pl.pallas_call(
        _linres_kernel,
        grid=(t // tm,),
        in_specs=[pl.BlockSpec((tm, k), lambda i: (i, 0)),
                  pl.BlockSpec((k, D), lambda i: (0, 0)),
                  pl.BlockSpec((tm, D), lambda i: (i, 0)),
                  _mod_spec(layer, kind, seg_fn)],
        out_specs=pl.BlockSpec((tm, D), lambda i: (i, 0)),
        out_shape=jax.ShapeDtypeStruct((t, D), F32),
        compiler_params=_cparams(1),
        name="linear_residual",
    )(y, w_bf16, x, mods)


def _residual_kernel(x_ref, y_ref, g_ref, o_ref):
    o_ref[...] = x_ref[...] + g_ref[...] * y_ref[...]


def _residual(x, y, mods, layer, kind, is_lat):
    t = x.shape[0]
    tm = 512
    seg_fn = _seg_of(is_lat, tm)
    return pl.pallas_call(
        _residual_kernel,
        grid=(t // tm,),
        in_specs=[pl.BlockSpec((tm, D), lambda i: (i, 0)),
                  pl.BlockSpec((tm, D), lambda i: (i, 0)),
                  _mod_spec(layer, kind, seg_fn)],
        out_specs=pl.BlockSpec((tm, D), lambda i: (i, 0)),
        out_shape=jax.ShapeDtypeStruct((t, D), F32),
        compiler_params=_cparams(1),
        name="gated_residual",
    )(x, y, mods)


def _final_norm_kernel(x_ref, g_ref, o_ref):
    x = x_ref[...]
    ms = jnp.mean(x * x, axis=-1, keepdims=True)
    o_ref[...] = x * lax.rsqrt(ms + EPS) * g_ref[...]


def _final_norm(x, g):
    t = x.shape[0]
    tm = 512
    return pl.pallas_call(
        _final_norm_kernel,
        grid=(t // tm,),
        in_specs=[pl.BlockSpec((tm, D), lambda i: (i, 0)), pl.BlockSpec((1, D), lambda i: (0, 0))],
        out_specs=pl.BlockSpec((tm, D), lambda i: (i, 0)),
        out_shape=jax.ShapeDtypeStruct((t, D), F32),
        compiler_params=_cparams(1),
        name="final_rmsnorm",
    )(x, g.reshape(1, D))


def _fold_kernel(c_ref, s_ref, w_ref, o_ref):
    w = w_ref[...]
    o_ref[:, :D] = _dot3(c_ref[...], w).astype(BF16)
    o_ref[:, D:] = _dot3(s_ref[...], w).astype(BF16)


def _fnet_fold(w):
    n = jnp.arange(FNET_GROUP_DIM, dtype=I32)
    ang = ((n[:, None] * n[None, :]) % FNET_GROUP_DIM).astype(F32) * (2.0 * math.pi / FNET_GROUP_DIM)
    sc = 1.0 / math.sqrt(FNET_GROUP_DIM)
    gd = FNET_GROUP_DIM
    return pl.pallas_call(
        _fold_kernel,
        grid=(D // gd,),
        in_specs=[pl.BlockSpec((gd, gd), lambda g: (0, 0)),
                  pl.BlockSpec((gd, gd), lambda g: (0, 0)),
                  pl.BlockSpec((gd, D), lambda g: (g, 0))],
        out_specs=pl.BlockSpec((gd, 2 * D), lambda g: (g, 0)),
        out_shape=jax.ShapeDtypeStruct((D, 2 * D), BF16),
        compiler_params=_cparams(1),
        name="fnet_fold_channel_dft",
    )(jnp.cos(ang) * sc, jnp.sin(ang) * sc, w)


def _seq_dft(length):
    n = jnp.arange(length, dtype=I32)
    ang = ((n[:, None] * n[None, :]) % length).astype(F32) * (2.0 * math.pi / length)
    sc = 1.0 / math.sqrt(length)
    return (jnp.cos(ang) * sc).astype(BF16), (jnp.sin(ang) * (-sc)).astype(BF16)


def _seqmix_kernel(c_ref, s_ref, y_ref, x_ref, g_ref, o_ref, acc_ref, *, nk):
    k = pl.program_id(2)

    @pl.when(k == 0)
    def _():
        acc_ref[...] = jnp.zeros_like(acc_ref)

    acc_ref[...] += _dot(c_ref[...], y_ref[:, :D]) + _dot(s_ref[...], y_ref[:, D:])

    @pl.when(k == nk - 1)
    def _():
        o_ref[...] = x_ref[...] + g_ref[...] * acc_ref[...]


def _seqmix(y, x, mods, layer, is_lat, batch, length):
    cm, sm = _seq_dft(length)
    tm = min(length, 1024)
    tk = min(length, 1024)
    nm, nk = length // tm, length // tk
    seg = (lambda b: 1 + b) if is_lat else (lambda b: 0)
    out = pl.pallas_call(
        functools.partial(_seqmix_kernel, nk=nk),
        grid=(batch, nm, nk),
        in_specs=[pl.BlockSpec((tm, tk), lambda b, m, k: (m, k)),
                  pl.BlockSpec((tm, tk), lambda b, m, k: (m, k)),
                  pl.BlockSpec((None, tk, 2 * D), lambda b, m, k: (b, k, 0)),
                  pl.BlockSpec((None, tm, D), lambda b, m, k: (b, m, 0)),
                  pl.BlockSpec((None, None, None, 1, D), lambda b, m, k: (layer, seg(b), 2, 0, 0))],
        out_specs=pl.BlockSpec((None, tm, D), lambda b, m, k: (b, m, 0)),
        out_shape=jax.ShapeDtypeStruct((batch, length, D), F32),
        scratch_shapes=[pltpu.VMEM((tm, D), F32)],
        compiler_params=_cparams(3),
        name="fnet_seq_dft",
    )(cm, sm, y.reshape(batch, length, 2 * D), x.reshape(batch, length, D), mods)
    return out.reshape(batch * length, D)


def _ret_tables(dec_f, dec_b):
    c = RET_C
    lgf = jax.nn.log_sigmoid(dec_f.astype(F32))[:, None, None]
    lgb = jax.nn.log_sigmoid(dec_b.astype(F32))[:, None, None]
    idx = jnp.arange(c, dtype=F32)
    diff = idx[:, None] - idx[None, :]
    kscale = RET_DK ** -0.5
    amat = jnp.where(diff >= 0, jnp.exp(jnp.maximum(diff, 0.0)[None] * lgf),
                     jnp.exp(jnp.maximum(-diff, 0.0)[None] * lgb)) * kscale
    col = idx[None, :, None]
    ones_k = jnp.ones((1, 1, RET_DK), F32)
    ones_v = jnp.ones((1, 1, RET_DV), F32)
    return dict(
        amat=amat,
        zf=jnp.exp((c - 1.0 - col) * lgf) * kscale * ones_k,
        zb=jnp.exp(col * lgb) * kscale * ones_k,
        xif=jnp.exp((col + 1.0) * lgf) * ones_v,
        xib=jnp.exp((c - col) * lgb) * ones_v,
        decf=jnp.exp(c * lgf) * ones_v,
        decb=jnp.exp(c * lgb) * ones_v,
    )


def _state_update(k_h, z_h, v_h):
    kz_t = jnp.transpose(k_h.astype(F32) * z_h).astype(BF16)
    return _dot(kz_t, v_h)


def _group_norm_gate(o, g_h):
    mu = jnp.mean(o, axis=-1, keepdims=True)
    d = o - mu
    var = jnp.mean(d * d, axis=-1, keepdims=True)
    g = g_h.astype(F32)
    return (_silu(g) * (d * lax.rsqrt(var + EPS))).astype(BF16)


def _ret_ctx_kernel(q_ref, k_ref, v_ref, g_ref, a_ref, zf_ref, zb_ref, y_ref, sf_ref, sb_ref):
    for h in range(RET_HEADS):
        q_h = q_ref[:, h * RET_DK:(h + 1) * RET_DK]
        k_h = k_ref[:, h * RET_DK:(h + 1) * RET_DK]
        v_h = v_ref[:, h * RET_DV:(h + 1) * RET_DV]
        inner = (_dot_nt(q_h, k_h) * a_ref[h]).astype(BF16)
        o = _dot(inner, v_h)
        y_ref[:, h * RET_DV:(h + 1) * RET_DV] = _group_norm_gate(o, g_ref[:, h * RET_DV:(h + 1) * RET_DV])
        sf_ref[h] = _state_update(k_h, zf_ref[h], v_h)
        sb_ref[h] = _state_update(k_h, zb_ref[h], v_h)


def _ret_ctx(proj, tabs, batch):
    c = RET_C
    qw, vw = RET_HEADS * RET_DK, RET_HEADS * RET_DV
    tab_spec = lambda arr: pl.BlockSpec(arr.shape, lambda b: (0, 0, 0))
    st_shape = jax.ShapeDtypeStruct((batch, RET_HEADS, RET_DK, RET_DV), F32)
    st_spec = pl.BlockSpec((None, RET_HEADS, RET_DK, RET_DV), lambda b: (b, 0, 0, 0))
    return pl.pallas_call(
        _ret_ctx_kernel,
        grid=(batch,),
        in_specs=[pl.BlockSpec((c, qw), lambda b: (b, 0)),
                  pl.BlockSpec((c, qw), lambda b: (b, 1)),
                  pl.BlockSpec((c, vw), lambda b: (b, 1)),
                  pl.BlockSpec((c, vw), lambda b: (b, 2)),
                  tab_spec(tabs["amat"]), tab_spec(tabs["zf"]), tab_spec(tabs["zb"])],
        out_specs=[pl.BlockSpec((c, vw), lambda b: (b, 0)), st_spec, st_spec],
        out_shape=[jax.ShapeDtypeStruct((batch * c, vw), BF16), st_shape, st_shape],
        compiler_params=_cparams(1),
        name="retention_context",
    )(proj, proj, proj, proj, tabs["amat"], tabs["zf"], tabs["zb"])


def _ret_bwd_kernel(q_ref, k_ref, v_ref, s0_ref, xib_ref, zb_ref, decb_ref, ob_ref, s_ref):
    @pl.when(pl.program_id(1) == 0)
    def _():
        s_ref[...] = s0_ref[...]

    for h in range(RET_HEADS):
        q_h = q_ref[:, h * RET_DK:(h + 1) * RET_DK]
        k_h = k_ref[:, h * RET_DK:(h + 1) * RET_DK]
        v_h = v_ref[:, h * RET_DV:(h + 1) * RET_DV]
        s = s_ref[h]
        ob_ref[:, h * RET_DV:(h + 1) * RET_DV] = _dot(q_h, s.astype(BF16)) * xib_ref[h]
        s_ref[h] = s * decb_ref[h] + _state_update(k_h, zb_ref[h], v_h)


def _ret_fwd_kernel(q_ref, k_ref, v_ref, g_ref, ob_ref, s0_ref, a_ref, xif_ref, zf_ref, decf_ref, y_ref, s_ref):
    @pl.when(pl.program_id(1) == 0)
    def _():
        s_ref[...] = s0_ref[...]

    for h in range(RET_HEADS):
        q_h = q_ref[:, h * RET_DK:(h + 1) * RET_DK]
        k_h = k_ref[:, h * RET_DK:(h + 1) * RET_DK]
        v_h = v_ref[:, h * RET_DV:(h + 1) * RET_DV]
        s = s_ref[h]
        inner = (_dot_nt(q_h, k_h) * a_ref[h]).astype(BF16)
        o = (_dot(inner, v_h) + _dot(q_h, s.astype(BF16)) * xif_ref[h]) + ob_ref[:, h * RET_DV:(h + 1) * RET_DV]
        y_ref[:, h * RET_DV:(h + 1) * RET_DV] = _group_norm_gate(o, g_ref[:, h * RET_DV:(h + 1) * RET_DV])
        s_ref[h] = s * decf_ref[h] + _state_update(k_h, zf_ref[h], v_h)


def _ret_lat(proj, tabs, s0_f, s0_b, batch, length):
    c = RET_C
    nc = length // c
    qw, vw = RET_HEADS * RET_DK, RET_HEADS * RET_DV
    t = batch * length
    tab_spec = lambda arr: pl.BlockSpec(arr.shape, lambda b, j: (0, 0, 0))
    st_spec = pl.BlockSpec((None, RET_HEADS, RET_DK, RET_DV), lambda b, j: (b, 0, 0, 0))
    st_scratch = pltpu.VMEM((RET_HEADS, RET_DK, RET_DV), F32)
    rev = lambda b, j: b * nc + (nc - 1 - j)
    fwd = lambda b, j: b * nc + j
    ob = pl.pallas_call(
        _ret_bwd_kernel,
        grid=(batch, nc),
        in_specs=[pl.BlockSpec((c, qw), lambda b, j: (rev(b, j), 0)),
                  pl.BlockSpec((c, qw), lambda b, j: (rev(b, j), 1)),
                  pl.BlockSpec((c, vw), lambda b, j: (rev(b, j), 1)),
                  st_spec, tab_spec(tabs["xib"]), tab_spec(tabs["zb"]), tab_spec(tabs["decb"])],
        out_specs=pl.BlockSpec((c, vw), lambda b, j: (rev(b, j), 0)),
        out_shape=jax.ShapeDtypeStruct((t, vw), F32),
        scratch_shapes=[st_scratch],
        compiler_params=_cparams(2),
        name="retention_latent_backward",
    )(proj, proj, proj, s0_b, tabs["xib"], tabs["zb"], tabs["decb"])
    return pl.pallas_call(
        _ret_fwd_kernel,
        grid=(batch, nc),
        in_specs=[pl.BlockSpec((c, qw), lambda b, j: (fwd(b, j), 0)),
                  pl.BlockSpec((c, qw), lambda b, j: (fwd(b, j), 1)),
                  pl.BlockSpec((c, vw), lambda b, j: (fwd(b, j), 1)),
                  pl.BlockSpec((c, vw), lambda b, j: (fwd(b, j), 2)),
                  pl.BlockSpec((c, vw), lambda b, j: (fwd(b, j), 0)),
                  st_spec, tab_spec(tabs["amat"]), tab_spec(tabs["xif"]), tab_spec(tabs["zf"]),
                  tab_spec(tabs["decf"])],
        out_specs=pl.BlockSpec((c, vw), lambda b, j: (fwd(b, j), 0)),
        out_shape=jax.ShapeDtypeStruct((t, vw), BF16),
        scratch_shapes=[st_scratch],
        compiler_params=_cparams(2),
        name="retention_latent_forward",
    )(proj, proj, proj, proj, ob, s0_f, tabs["amat"], tabs["xif"], tabs["zf"], tabs["decf"])


def _attn_ctx_kernel(sink_ref, q_ref, k_ref, v_ref, o_ref):
    scale = HEAD_DIM ** -0.5
    outs = []
    for g in range(ATTN_KV):
        k_g = k_ref[:, g * HEAD_DIM:(g + 1) * HEAD_DIM].astype(BF16)
        v_g = v_ref[:, g * HEAD_DIM:(g + 1) * HEAD_DIM].astype(BF16)
        for r in range(ATTN_REP):
            hh = g * ATTN_REP + r
            q_h = q_ref[:, hh * HEAD_DIM:(hh + 1) * HEAD_DIM].astype(BF16)
            s = _dot_nt(q_h, k_g) * scale
            snk = sink_ref[hh]
            m = jnp.maximum(jnp.max(s, axis=-1, keepdims=True), snk)
            p = jnp.exp(s - m)
            den = jnp.sum(p, axis=-1, keepdims=True) + jnp.exp(snk - m)
            outs.append(_dot(p.astype(BF16), v_g) / den)
    o_ref[...] = jnp.concatenate(outs, axis=1).astype(o_ref.dtype)


def _attn_ctx(qkv, sink, batch, length):
    qw = ATTN_HEADS * HEAD_DIM
    kw = ATTN_KV * HEAD_DIM
    return pl.pallas_call(
        _attn_ctx_kernel,
        grid=(batch,),
        in_specs=[pl.BlockSpec(memory_space=pltpu.SMEM),
                  pl.BlockSpec((length, qw), lambda b: (b, 0)),
                  pl.BlockSpec((length, kw), lambda b: (b, qw // kw)),
                  pl.BlockSpec((length, kw), lambda b: (b, qw // kw + 1))],
        out_specs=pl.BlockSpec((length, qw), lambda b: (b, 0)),
        out_shape=jax.ShapeDtypeStruct((batch * length, qw), BF16),
        compiler_params=_cparams(1),
        name="attention_context",
    )(sink, qkv, qkv, qkv)


def _attn_lat_kernel(sink_ref, q_ref, k_ref, v_ref, kc_ref, vc_ref, o_ref, *, length):
    scale = HEAD_DIM ** -0.5
    band = 3 * QBLK
    j = pl.program_id(1)
    start = pl.multiple_of(jnp.clip((j - 1) * QBLK, 0, length - band), QBLK)
    rows = ATTN_REP * QBLK
    qpos = j * QBLK + lax.broadcasted_iota(I32, (rows, band), 0) % QBLK
    kpos = start + lax.broadcasted_iota(I32, (rows, band), 1)
    valid = jnp.abs(qpos - kpos) <= WINDOW
    rep_of_row = lax.broadcasted_iota(I32, (rows, 1), 0) // QBLK
    kb = k_ref[pl.ds(start, band), :]
    vb = v_ref[pl.ds(start, band), :]
    outs = []
    for g in range(ATTN_KV):
        cs = slice(g * HEAD_DIM, (g + 1) * HEAD_DIM)
        k_g = kb[:, cs].astype(BF16)
        v_g = vb[:, cs].astype(BF16)
        kc_g = kc_ref[:, cs].astype(BF16)
        vc_g = vc_ref[:, cs].astype(BF16)
        q_g = jnp.concatenate(
            [q_ref[:, (g * ATTN_REP + r) * HEAD_DIM:(g * ATTN_REP + r + 1) * HEAD_DIM] for r in range(ATTN_REP)],
            axis=0).astype(BF16)
        snk = jnp.zeros((rows, 1), F32)
        for r in range(ATTN_REP):
            snk = jnp.where(rep_of_row == r, sink_ref[g * ATTN_REP + r], snk)
        s_loc = jnp.where(valid, _dot_nt(q_g, k_g) * scale, NEG_INF)
        s_ctx = _dot_nt(q_g, kc_g) * scale
        m = jnp.maximum(jnp.maximum(jnp.max(s_loc, axis=-1, keepdims=True),
                                    jnp.max(s_ctx, axis=-1, keepdims=True)), snk)
        p_loc = jnp.exp(s_loc - m)
        p_ctx = jnp.exp(s_ctx - m)
        den = (jnp.sum(p_loc, axis=-1, keepdims=True) + jnp.sum(p_ctx, axis=-1, keepdims=True)) + jnp.exp(snk - m)
        o_g = (_dot(p_loc.astype(BF16), v_g) + _dot(p_ctx.astype(BF16), vc_g)) / den
        for r in range(ATTN_REP):
            outs.append(o_g[r * QBLK:(r + 1) * QBLK, :])
    o_ref[...] = jnp.concatenate(outs, axis=1).astype(o_ref.dtype)


def _attn_lat(qkv, sink, kc, vc, batch, length):
    qw = ATTN_HEADS * HEAD_DIM
    kw = ATTN_KV * HEAD_DIM
    nq = length // QBLK
    past = kc.shape[1]
    return pl.pallas_call(
        functools.partial(_attn_lat_kernel, length=length),
        grid=(batch, nq),
        in_specs=[pl.BlockSpec(memory_space=pltpu.SMEM),
                  pl.BlockSpec((QBLK, qw), lambda b, j: (b * nq + j, 0)),
                  pl.BlockSpec((length, kw), lambda b, j: (b, qw // kw)),
                  pl.BlockSpec((length, kw), lambda b, j: (b, qw // kw + 1)),
                  pl.BlockSpec((None, past, kw), lambda b, j: (b, 0, 0)),
                  pl.BlockSpec((None, past, kw), lambda b, j: (b, 0, 0))],
        out_specs=pl.BlockSpec((QBLK, qw), lambda b, j: (b * nq + j, 0)),
        out_shape=jax.ShapeDtypeStruct((batch * length, qw), BF16),
        compiler_params=_cparams(2),
        name="attention_latent",
    )(sink, qkv, qkv, qkv, kc, vc)


def _rope_tables(length):
    axis_dim = HEAD_DIM // 2
    pos = jnp.arange(length)
    row = (pos // GRID_W).astype(F32)
    colp = (pos % GRID_W).astype(F32)
    inv = ROPE_BASE ** (-jnp.arange(0, axis_dim, 2, dtype=F32) / axis_dim)
    ar = row[:, None] * inv[None, :]
    ac = colp[:, None] * inv[None, :]
    cos = jnp.concatenate([jnp.cos(ar), jnp.cos(ar), jnp.cos(ac), jnp.cos(ac)], axis=1)
    sin = jnp.concatenate([-jnp.sin(ar), jnp.sin(ar), -jnp.sin(ac), jnp.sin(ac)], axis=1)
    return jnp.concatenate([cos, cos], axis=1), jnp.concatenate([sin, sin], axis=1)


def _router_kernel(x_ref, g_ref, sh_ref, sc_ref, wr_ref, br_ref, h_ref, e_ref, p_ref):
    h = _norm_mod(x_ref[...], g_ref[...], sh_ref[...], sc_ref[...])
    h_ref[...] = h
    logits = _dot3(h, wr_ref[...]) + br_ref[...]
    lane = lax.broadcasted_iota(I32, logits.shape, 1).astype(F32)
    ninf = -jnp.inf
    lg = jnp.where(lane < N_GROUPS, logits, ninf)
    gmax = jnp.max(lg, axis=-1, keepdims=True)
    gtop = jnp.min(jnp.where(lg == gmax, lane, float(LANE)), axis=-1, keepdims=True)
    pg_sel = 1.0 / jnp.sum(jnp.exp(lg - gmax), axis=-1, keepdims=True)
    lo = N_GROUPS + EXP_PER_GROUP * gtop
    le = jnp.where((lane >= lo) & (lane < lo + EXP_PER_GROUP), logits, ninf)
    v1 = jnp.max(le, axis=-1, keepdims=True)
    i1 = jnp.min(jnp.where(le == v1, lane, float(LANE)), axis=-1, keepdims=True)
    le2 = jnp.where(lane == i1, ninf, le)
    v2 = jnp.max(le2, axis=-1, keepdims=True)
    i2 = jnp.min(jnp.where(le2 == v2, lane, float(LANE)), axis=-1, keepdims=True)
    t = jnp.exp(v2 - v1)
    gate1 = pg_sel * (1.0 / (1.0 + t))
    gate2 = pg_sel * (t / (1.0 + t))
    e_ref[...] = jnp.where(lane == 0, i1 - N_GROUPS, jnp.where(lane == 1, i2 - N_GROUPS, 0.0)).astype(I32)
    p_ref[...] = jnp.where(lane == 0, gate1, jnp.where(lane == 1, gate2, 0.0))


def _router(x, mods, layer, norm_g, wr, br, is_lat):
    t = x.shape[0]
    tm = 256
    seg_fn = _seg_of(is_lat, tm)
    return pl.pallas_call(
        _router_kernel,
        grid=(t // tm,),
        in_specs=[pl.BlockSpec((tm, D), lambda i: (i, 0)),
                  pl.BlockSpec((1, D), lambda i: (0, 0)),
                  _mod_spec(layer, 3, seg_fn),
                  _mod_spec(layer, 4, seg_fn),
                  pl.BlockSpec((D, LANE), lambda i: (0, 0)),
                  pl.BlockSpec((1, LANE), lambda i: (0, 0))],
        out_specs=[pl.BlockSpec((tm, D), lambda i: (i, 0)),
                   pl.BlockSpec((tm, LANE), lambda i: (i, 0)),
                   pl.BlockSpec((tm, LANE), lambda i: (i, 0))],
        out_shape=[jax.ShapeDtypeStruct((t, D), F32),
                   jax.ShapeDtypeStruct((t, LANE), I32),
                   jax.ShapeDtypeStruct((t, LANE), F32)],
        compiler_params=_cparams(1),
        name="moe_router",
    )(x, norm_g.reshape(1, D), mods, mods, wr, br)


def _moe_kernel(src_ref, meta_ref, h_hbm, gates_ref, w1_ref, w3_ref, w2_ref, y_hbm,
                h_vm, y_vm, xt, ot, w1b, w3b, w2b, sem, *, tc, a_pad):
    chunk = pl.program_id(0)
    e = pl.program_id(1)
    stride = MOE_TILE_STRIDE
    n_lane_chunks = D // LANE
    h_rows = tc * SUB
    src_base = chunk * a_pad

    def h_copy():
        return pltpu.make_async_copy(h_hbm.at[pl.ds(chunk * h_rows, h_rows)], h_vm.at[pl.ds(0, h_rows)], sem.at[0])

    def y_copy():
        return pltpu.make_async_copy(y_vm.at[pl.ds(0, h_rows)], y_hbm.at[pl.ds(chunk * h_rows, h_rows)], sem.at[1])

    @pl.when(e == 0)
    def _():
        h_copy().start()

        def zero(i, carry):
            y_vm[pl.ds(pl.multiple_of(i * 512, 512), 512), :] = jnp.zeros((512, LANE), F32)
            return carry

        lax.fori_loop(0, h_rows // 512, zero, 0)
        y_vm[pl.ds(h_rows, SUB), :] = jnp.zeros((SUB, LANE), F32)
        h_vm[pl.ds(h_rows, SUB), :] = jnp.zeros((SUB, LANE), F32)
        h_copy().wait()

    w1b[...] = w1_ref[...].astype(BF16)
    w3b[...] = w3_ref[...].astype(BF16)
    w2b[...] = w2_ref[...].astype(BF16)

    start = meta_ref[chunk * 2 * N_EXP + e]
    n_pad = meta_ref[chunk * 2 * N_EXP + N_EXP + e]

    def process(off, m):
        off = pl.multiple_of(off, MOE_PAD)

        def gather(i, carry):
            for u in range(SUB):
                r = i * SUB + u
                t8 = pl.multiple_of(src_ref[src_base + off + r], SUB)
                xt[pl.ds(r, SUB, stride=stride), :] = h_vm[pl.ds(t8, SUB), :]
            return carry

        lax.fori_loop(0, m // SUB, gather, 0)
        x = jnp.concatenate([xt[pl.ds(c * stride, m), :] for c in range(n_lane_chunks)], axis=1).astype(BF16)
        a = _dot(x, w1b[...])
        b = _dot(x, w3b[...])
        hm = (_silu(a) * b).astype(BF16)
        out = _dot(hm, w2b[...])
        gq = gates_ref[pl.ds(off // MOE_PAD, m // MOE_PAD)]
        eye = (lax.broadcasted_iota(I32, (MOE_PAD, MOE_PAD), 0) ==
               lax.broadcasted_iota(I32, (MOE_PAD, MOE_PAD), 1))
        gcol = jnp.concatenate(
            [jnp.sum(jnp.where(eye, gq[a_], 0.0), axis=1, keepdims=True) for a_ in range(m // MOE_PAD)],
            axis=0)
        out = out * gcol
        for c in range(n_lane_chunks):
            ot[pl.ds(c * stride, m), :] = out[:, c * LANE:(c + 1) * LANE]

        def scatter(i, carry):
            t8s, news = [], []
            for u in range(4):
                r = i * 4 + u
                t8 = pl.multiple_of(src_ref[src_base + off + r], SUB)
                t8s.append(t8)
                news.append(y_vm[pl.ds(t8, SUB), :] + ot[pl.ds(r, SUB, stride=stride), :])
            for u in range(4):
                y_vm[pl.ds(t8s[u], SUB), :] = news[u]
            return carry

        lax.fori_loop(0, m // 4, scatter, 0)

    n_full = n_pad // 256

    def full_body(i, carry):
        process(start + i * 256, 256)
        return carry

    lax.fori_loop(0, n_full, full_body, 0)
    rem = start + n_full * 256

    @pl.when((n_pad & 128) != 0)
    def _():
        process(rem, 128)

    @pl.when((n_pad & 64) != 0)
    def _():
        process(rem + (n_pad & 128), 64)

    @pl.when(e == N_EXP - 1)
    def _():
        y_copy().start()
        y_copy().wait()


def _moe_plan(experts, gates, tc):
    a = tc * 2
    a_pad = a + N_EXP * MOE_PAD
    e = experts.reshape(a)
    g = gates.reshape(a)
    tok = jnp.arange(a, dtype=I32) // 2
    onehot = (e[:, None] == jnp.arange(N_EXP, dtype=I32)[None, :]).astype(I32)
    csum = jnp.cumsum(onehot, axis=0)
    rank = jnp.sum(onehot * csum, axis=1) - 1
    counts = csum[-1]
    padded = (counts + MOE_PAD - 1) // MOE_PAD * MOE_PAD
    pad_start = jnp.cumsum(padded) - padded
    dest = pad_start[e] + rank
    src8 = jnp.full((a_pad,), tc * SUB, I32).at[dest].set(tok * SUB)
    gsort = jnp.zeros((a_pad,), F32).at[dest].set(g)
    meta = jnp.concatenate([pad_start, padded]).astype(I32)
    return src8, meta, gsort.reshape(a_pad // MOE_PAD, 1, MOE_PAD)


def _moe(h, experts, gates, w1, w3, w2, layer):
    t = h.shape[0]
    tc = SEG
    n_chunks = t // tc
    a_pad = 2 * tc + N_EXP * MOE_PAD
    src8, meta, gq = jax.vmap(functools.partial(_moe_plan, tc=tc))(
        experts.reshape(n_chunks, tc, 2), gates.reshape(n_chunks, tc, 2))
    units = a_pad // MOE_PAD
    slab_rows = (tc + 1) * SUB
    tile_rows = (D // LANE) * MOE_TILE_STRIDE
    grid_spec = pltpu.PrefetchScalarGridSpec(
        num_scalar_prefetch=2,
        grid=(n_chunks, N_EXP),
        in_specs=[pl.BlockSpec(memory_space=pl.ANY),
                  pl.BlockSpec((None, units, 1, MOE_PAD), lambda c, e, *_: (c, 0, 0, 0)),
                  pl.BlockSpec((None, None, D, D_FF), lambda c, e, *_: (layer, e, 0, 0)),
                  pl.BlockSpec((None, None, D, D_FF), lambda c, e, *_: (layer, e, 0, 0)),
                  pl.BlockSpec((None, None, D_FF, D), lambda c, e, *_: (layer, e, 0, 0))],
        out_specs=pl.BlockSpec(memory_space=pl.ANY),
        scratch_shapes=[pltpu.VMEM((slab_rows, LANE), F32),
                        pltpu.VMEM((slab_rows, LANE), F32),
                        pltpu.VMEM((tile_rows, LANE), F32),
                        pltpu.VMEM((tile_rows, LANE), F32),
                        pltpu.VMEM((D, D_FF), BF16),
                        pltpu.VMEM((D, D_FF), BF16),
                        pltpu.VMEM((D_FF, D), BF16),
                        pltpu.SemaphoreType.DMA((2,))],
    )
    y2d = pl.pallas_call(
        functools.partial(_moe_kernel, tc=tc, a_pad=a_pad),
        grid_spec=grid_spec,
        out_shape=jax.ShapeDtypeStruct((t * SUB, LANE), F32),
        compiler_params=_cparams(2),
        name="moe_experts",
    )(src8.reshape(n_chunks * a_pad), meta.reshape(n_chunks * 2 * N_EXP), h.reshape(t * SUB, LANE), gq, w1, w3, w2)
    return y2d.reshape(t, D)


def kernel(x_prompt, x_sample, c, cache_attn_k, cache_attn_v, state_ret_fwd, state_ret_bwd, c_ctx, adaln_w, adaln_b, norm1_g, norm2_g, final_g, fnet_w, ret_w_in, ret_w_out, ret_decay_fwd, ret_decay_bwd, attn_w_in, attn_w_out, attn_sink, router_group_w, router_group_b, router_expert_w, router_expert_b, moe_w1, moe_w3, moe_w2):
    bp, lp, _ = x_prompt.shape
    bs, ls, _ = x_sample.shape
    xs = {False: x_prompt.reshape(bp * lp, D), True: x_sample.reshape(bs * ls, D)}
    dims = {False: (bp, lp), True: (bs, ls)}

    cond8 = jnp.concatenate([c_ctx[None, :], c, jnp.zeros((SUB - 1 - bs, D), F32)], axis=0)
    mods = _adaln(cond8, adaln_w, adaln_b).reshape(DEPTH, SUB, 6, 1, D)

    rope_lat = _rope_tables(ls)
    rope = {False: (jnp.ones((bp * lp, LANE), F32), jnp.zeros((bp * lp, LANE), F32)),
            True: (jnp.tile(rope_lat[0], (bs, 1)), jnp.tile(rope_lat[1], (bs, 1)))}

    extras = {}
    for i in range(DEPTH):
        kind, j = i % 3, i // 3
        if kind == 0:
            lw = dict(w_fold=_fnet_fold(fnet_w[j]))
        elif kind == 1:
            lw = dict(w_in=ret_w_in[j].astype(BF16), w_out=ret_w_out[j].astype(BF16),
                      tabs=_ret_tables(ret_decay_fwd[j], ret_decay_bwd[j]),
                      s0_f=state_ret_fwd[:, j], s0_b=state_ret_bwd[:, j])
        else:
            kvw = ATTN_KV * HEAD_DIM
            lw = dict(w_in=attn_w_in[j].astype(BF16), w_out=attn_w_out[j].astype(BF16), sink=attn_sink[j],
                      kc=cache_attn_k[:, j].reshape(bs, -1, kvw), vc=cache_attn_v[:, j].reshape(bs, -1, kvw))
        wr = jnp.concatenate([router_group_w[i], jnp.transpose(router_expert_w[i], (1, 0, 2)).reshape(D, N_EXP),
                              jnp.zeros((D, LANE - N_GROUPS - N_EXP), F32)], axis=1)
        br = jnp.concatenate([router_group_b[i], router_expert_b[i].reshape(N_EXP),
                              jnp.zeros((LANE - N_GROUPS - N_EXP,), F32)]).reshape(1, LANE)

        for is_lat in (False, True):
            b_, l_ = dims[is_lat]
            x = _mixer(i, xs[is_lat], mods, norm1_g[i], lw, is_lat, b_, l_, rope[is_lat], extras)
            xs[is_lat] = _ffn(i, x, mods, norm2_g[i], wr, br, moe_w1, moe_w3, moe_w2, is_lat)

    y_prompt = _final_norm(xs[False], final_g).reshape(bp, lp, D)
    y_sample = _final_norm(xs[True], final_g).reshape(bs, ls, D)
    return (y_prompt, y_sample, extras["k"], extras["v"], extras["sf"], extras["sb"])


def _mixer(i, x, mods, norm_g, lw, is_lat, b_, l_, rope, extras):
    kind = i % 3
    if kind == 0:
        y = _proj(x, mods, i, 0, norm_g, lw["w_fold"], is_lat, BF16)
        return _seqmix(y, x, mods, i, is_lat, b_, l_)
    if kind == 1:
        proj = _proj(x, mods, i, 0, norm_g, lw["w_in"], is_lat, BF16)
        if is_lat:
            yr = _ret_lat(proj, lw["tabs"], lw["s0_f"], lw["s0_b"], b_, l_)
        else:
            yr, sf, sb = _ret_ctx(proj, lw["tabs"], b_)
            extras["sf"], extras["sb"] = sf[:, None], sb[:, None]
        return _linres(yr, lw["w_out"], x, mods, i, 2, is_lat)
    qkv = _proj(x, mods, i, 0, norm_g, lw["w_in"], is_lat, F32, rope=rope)
    if is_lat:
        o = _attn_lat(qkv, lw["sink"], lw["kc"], lw["vc"], b_, l_)
    else:
        qw = ATTN_HEADS * HEAD_DIM
        kw = ATTN_KV * HEAD_DIM
        extras["k"] = qkv[:, qw:qw + kw].reshape(b_, 1, l_, ATTN_KV, HEAD_DIM)
        extras["v"] = qkv[:, qw + kw:].reshape(b_, 1, l_, ATTN_KV, HEAD_DIM)
        o = _attn_ctx(qkv, lw["sink"], b_, l_)
    return _linres(o, lw["w_out"], x, mods, i, 2, is_lat)


def _ffn(i, x, mods, norm_g, wr, br, w1, w3, w2, is_lat):
    h, e_out, p_out = _router(x, mods, i, norm_g, wr, br, is_lat)
    y = _moe(h, e_out[:, :2], p_out[:, :2], w1, w3, w2, i)
    return _residual(x, y, mods, i, 5, is_lat)
```

```python
import functools
import math

import jax
import jax.numpy as jnp
from jax import lax
from jax.experimental import pallas as pl
from jax.experimental.pallas import tpu as pltpu

F32 = jnp.float32
BF16 = jnp.bfloat16
I32 = jnp.int32

D = 1024
DEPTH = 4
SEG = 4096
FNET_GROUP_DIM = 256
RET_HEADS, RET_DK, RET_DV = 4, 256, 512
RET_C = 256
ATTN_HEADS, ATTN_KV, HEAD_DIM, ATTN_REP = 16, 4, 64, 4
WINDOW, QBLK = 128, 128
GRID_W = 64
ROPE_BASE = 10000.0
N_EXP, EXP_PER_GROUP, N_GROUPS, D_FF = 32, 8, 4, 512
EPS = 1e-6
NEG_INF = -1e30
LANE = 128
SUB = 8

V7X_VMEM_LIMIT = 56 * 1024 * 1024
V7X_VMEM_LIMIT_MOE = 59 * 1024 * 1024
MOE_PAD = 64
MOE_TILE_STRIDE = 264


def _cparams(n_axes):
    return pltpu.CompilerParams(dimension_semantics=("arbitrary",) * n_axes, vmem_limit_bytes=V7X_VMEM_LIMIT)


def _dot(a, b):
    return jnp.dot(a, b, preferred_element_type=F32)


def _dot_nt(a, b):
    return lax.dot_general(a, b, (((1,), (1,)), ((), ())), preferred_element_type=F32)


def _split(x):
    hi = x.astype(BF16)
    lo = (x - hi.astype(F32)).astype(BF16)
    return hi, lo


def _dot3(a, b):
    ah, al = _split(a)
    bh, bl = _split(b)
    return _dot(ah, bh) + (_dot(ah, bl) + _dot(al, bh))


def _silu(x):
    return x * jax.nn.sigmoid(x)


def _norm_mod(x, g, sh, sc):
    ms = jnp.mean(x * x, axis=-1, keepdims=True)
    return (x * lax.rsqrt(ms + EPS) * g) * (1.0 + sc) + sh


def _seg_of(is_lat, tm):
    if is_lat:
        return lambda i: 1 + (i * tm) // SEG
    return lambda i: 0


def _mod_spec(layer, kind, seg_fn):
    return pl.BlockSpec((None, None, None, 1, D), lambda i, *_: (layer, seg_fn(i), kind, 0, 0))


def _adaln_kernel(c_ref, w_ref, b_ref, o_ref):
    o_ref[...] = _dot3(_silu(c_ref[...]), w_ref[...]) + b_ref[...]


def _adaln(cond8, w, b):
    depth, _, n = w.shape
    tn = 1536
    return pl.pallas_call(
        _adaln_kernel,
        grid=(depth, n // tn),
        in_specs=[pl.BlockSpec((SUB, D), lambda l, j: (0, 0)),
                  pl.BlockSpec((None, D, tn), lambda l, j: (l, 0, j)),
                  pl.BlockSpec((None, 1, tn), lambda l, j: (l, 0, j))],
        out_specs=pl.BlockSpec((None, SUB, tn), lambda l, j: (l, 0, j)),
        out_shape=jax.ShapeDtypeStruct((depth, SUB, n), F32),
        compiler_params=_cparams(2),
        name="adaln",
    )(cond8, w, b.reshape(depth, 1, n))


def _rope(x, cos_t, sin_t):
    w = x.shape[1]
    reps = w // LANE
    c = jnp.concatenate([cos_t] * reps, axis=1)
    s = jnp.concatenate([sin_t] * reps, axis=1)
    lane = lax.broadcasted_iota(I32, x.shape, 1)
    first = (lane % 32) < 16
    partner = jnp.where(first, pltpu.roll(x, w - 16, 1), pltpu.roll(x, 16, 1))
    return x * c + partner * s


def _proj_kernel(x_ref, g_ref, sh_ref, sc_ref, w_ref, *rest, n_rope):
    if n_rope:
        cos_ref, sin_ref, o_ref = rest
    else:
        (o_ref,) = rest
    hb = _norm_mod(x_ref[...], g_ref[...], sh_ref[...], sc_ref[...]).astype(BF16)
    n = w_ref.shape[1]
    tn = 512
    if n_rope:
        q_w = ATTN_HEADS * HEAD_DIM
        k_w = ATTN_KV * HEAD_DIM
        q = _dot(hb, w_ref[:, :q_w])
        o_ref[:, :q_w] = _rope(q, cos_ref[...], sin_ref[...])
        k = _dot(hb, w_ref[:, q_w:q_w + k_w])
        o_ref[:, q_w:q_w + k_w] = _rope(k, cos_ref[...], sin_ref[...])
        o_ref[:, q_w + k_w:] = _dot(hb, w_ref[:, q_w + k_w:])
    else:
        for j in range(n // tn):
            o_ref[:, j * tn:(j + 1) * tn] = _dot(hb, w_ref[:, j * tn:(j + 1) * tn]).astype(o_ref.dtype)


def _proj(x, mods, layer, sub, norm_g, w_bf16, is_lat, out_dtype, rope=None):
    t = x.shape[0]
    n = w_bf16.shape[1]
    tm = 256
    seg_fn = _seg_of(is_lat, tm)
    in_specs = [pl.BlockSpec((tm, D), lambda i: (i, 0)),
                pl.BlockSpec((1, D), lambda i: (0, 0)),
                _mod_spec(layer, 3 * sub + 0, seg_fn),
                _mod_spec(layer, 3 * sub + 1, seg_fn),
                pl.BlockSpec((D, n), lambda i: (0, 0))]
    args = [x, norm_g.reshape(1, D), mods, mods, w_bf16]
    if rope is not None:
        in_specs += [pl.BlockSpec((tm, LANE), lambda i: (i, 0))] * 2
        args += list(rope)
    return pl.pallas_call(
        functools.partial(_proj_kernel, n_rope=rope is not None),
        grid=(t // tm,),
        in_specs=in_specs,
        out_specs=pl.BlockSpec((tm, n), lambda i: (i, 0)),
        out_shape=jax.ShapeDtypeStruct((t, n), out_dtype),
        compiler_params=_cparams(1),
        name="norm_mod_proj",
    )(*args)


def _linres_kernel(y_ref, w_ref, x_ref, g_ref, o_ref):
    o_ref[...] = x_ref[...] + g_ref[...] * _dot(y_ref[...].astype(BF16), w_ref[...])


def _linres(y, w_bf16, x, mods, layer, kind, is_lat):
    t, k = y.shape
    tm = 256
    seg_fn = _seg_of(is_lat, tm)
    return pl.pallas_call(
        _linres_kernel,
        grid=(t // tm,),
        in_specs=[pl.BlockSpec((tm, k), lambda i: (i, 0)),
                  pl.BlockSpec((k, D), lambda i: (0, 0)),
                  pl.BlockSpec((tm, D), lambda i: (i, 0)),
                  _mod_spec(layer, kind, seg_fn)],
        out_specs=pl.BlockSpec((tm, D), lambda i: (i, 0)),
        out_shape=jax.ShapeDtypeStruct((t, D), F32),
        compiler_params=_cparams(1),
        name="linear_residual",
    )(y, w_bf16, x, mods)


def _to_slabs(slab_ref, x):
    tm = x.shape[0]
    for c in range(D // LANE):
        slab_ref[pl.ds(c, tm, stride=SUB), :] = x[:, c * LANE:(c + 1) * LANE]


def _from_slabs(slab_ref, tm):
    return jnp.concatenate([slab_ref[pl.ds(c, tm, stride=SUB), :] for c in range(D // LANE)], axis=1)


def _residual_kernel(x_ref, y_ref, g_ref, o_ref):
    o_ref[...] = x_ref[...] + g_ref[...] * _from_slabs(y_ref, x_ref.shape[0])


def _residual(x, y_slabs, mods, layer, kind, is_lat):
    t = x.shape[0]
    tm = 512
    seg_fn = _seg_of(is_lat, tm)
    return pl.pallas_call(
        _residual_kernel,
        grid=(t // tm,),
        in_specs=[pl.BlockSpec((tm, D), lambda i: (i, 0)),
                  pl.BlockSpec((tm * SUB, LANE), lambda i: (i, 0)),
                  _mod_spec(layer, kind, seg_fn)],
        out_specs=pl.BlockSpec((tm, D), lambda i: (i, 0)),
        out_shape=jax.ShapeDtypeStruct((t, D), F32),
        compiler_params=_cparams(1),
        name="gated_residual",
    )(x, y_slabs, mods)


def _final_norm_kernel(x_ref, g_ref, o_ref):
    x = x_ref[...]
    ms = jnp.mean(x * x, axis=-1, keepdims=True)
    o_ref[...] = x * lax.rsqrt(ms + EPS) * g_ref[...]


def _final_norm(x, g):
    t = x.shape[0]
    tm = 512
    return pl.pallas_call(
        _final_norm_kernel,
        grid=(t // tm,),
        in_specs=[pl.BlockSpec((tm, D), lambda i: (i, 0)), pl.BlockSpec((1, D), lambda i: (0, 0))],
        out_specs=pl.BlockSpec((tm, D), lambda i: (i, 0)),
        out_shape=jax.ShapeDtypeStruct((t, D), F32),
        compiler_params=_cparams(1),
        name="final_rmsnorm",
    )(x, g.reshape(1, D))


def _fold_kernel(c_ref, s_ref, w_ref, o_ref):
    w = w_ref[...]
    o_ref[:, :D] = _dot3(c_ref[...], w).astype(BF16)
    o_ref[:, D:] = _dot3(s_ref[...], w).astype(BF16)


def _fnet_fold(w):
    n = jnp.arange(FNET_GROUP_DIM, dtype=I32)
    ang = ((n[:, None] * n[None, :]) % FNET_GROUP_DIM).astype(F32) * (2.0 * math.pi / FNET_GROUP_DIM)
    sc = 1.0 / math.sqrt(FNET_GROUP_DIM)
    gd = FNET_GROUP_DIM
    return pl.pallas_call(
        _fold_kernel,
        grid=(D // gd,),
        in_specs=[pl.BlockSpec((gd, gd), lambda g: (0, 0)),
                  pl.BlockSpec((gd, gd), lambda g: (0, 0)),
                  pl.BlockSpec((gd, D), lambda g: (g, 0))],
        out_specs=pl.BlockSpec((gd, 2 * D), lambda g: (g, 0)),
        out_shape=jax.ShapeDtypeStruct((D, 2 * D), BF16),
        compiler_params=_cparams(1),
        name="fnet_fold_channel_dft",
    )(jnp.cos(ang) * sc, jnp.sin(ang) * sc, w)


def _seq_dft(length):
    n = jnp.arange(length, dtype=I32)
    sc = 1.0 / math.sqrt(length)
    split = 64
    if length <= split * 4:
        ang = ((n[:, None] * n[None, :]) % length).astype(F32) * (2.0 * math.pi / length)
        return (jnp.cos(ang) * sc).astype(BF16), (jnp.sin(ang) * (-sc)).astype(BF16)
    hi = jnp.arange(length // split, dtype=I32)
    lo = jnp.arange(split, dtype=I32)
    ang_a = ((n[:, None] * hi[None, :]) % (length // split)).astype(F32) * (2.0 * math.pi * split / length)
    ang_b = ((n[:, None] * lo[None, :]) % length).astype(F32) * (2.0 * math.pi / length)
    ca, sa = jnp.cos(ang_a)[:, :, None], jnp.sin(ang_a)[:, :, None]
    cb, sb = (jnp.cos(ang_b) * sc)[:, None, :], (jnp.sin(ang_b) * sc)[:, None, :]
    cm = (ca * cb - sa * sb).reshape(length, length)
    sm = (sa * cb + ca * sb).reshape(length, length)
    return cm.astype(BF16), (-sm).astype(BF16)


def _seqmix_kernel(c_ref, s_ref, y_ref, x_ref, g_ref, o_ref, acc_ref, *, nk):
    k = pl.program_id(2)

    @pl.when(k == 0)
    def _():
        acc_ref[...] = jnp.zeros_like(acc_ref)

    acc_ref[...] += _dot(c_ref[...], y_ref[:, :D]) + _dot(s_ref[...], y_ref[:, D:])

    @pl.when(k == nk - 1)
    def _():
        o_ref[...] = x_ref[...] + g_ref[...] * acc_ref[...]


def _seqmix(y, x, mods, layer, is_lat, batch, length):
    cm, sm = _seq_dft(length)
    tm = min(length, 1024)
    tk = min(length, 1024)
    nm, nk = length // tm, length // tk
    seg = (lambda b: 1 + b) if is_lat else (lambda b: 0)
    out = pl.pallas_call(
        functools.partial(_seqmix_kernel, nk=nk),
        grid=(batch, nm, nk),
        in_specs=[pl.BlockSpec((tm, tk), lambda b, m, k: (m, k)),
                  pl.BlockSpec((tm, tk), lambda b, m, k: (m, k)),
                  pl.BlockSpec((None, tk, 2 * D), lambda b, m, k: (b, k, 0)),
                  pl.BlockSpec((None, tm, D), lambda b, m, k: (b, m, 0)),
                  pl.BlockSpec((None, None, None, 1, D), lambda b, m, k: (layer, seg(b), 2, 0, 0))],
        out_specs=pl.BlockSpec((None, tm, D), lambda b, m, k: (b, m, 0)),
        out_shape=jax.ShapeDtypeStruct((batch, length, D), F32),
        scratch_shapes=[pltpu.VMEM((tm, D), F32)],
        compiler_params=_cparams(3),
        name="fnet_seq_dft",
    )(cm, sm, y.reshape(batch, length, 2 * D), x.reshape(batch, length, D), mods)
    return out.reshape(batch * length, D)


def _ret_tables(dec_f, dec_b):
    c = RET_C
    lgf = jax.nn.log_sigmoid(dec_f.astype(F32))[:, None, None]
    lgb = jax.nn.log_sigmoid(dec_b.astype(F32))[:, None, None]
    idx = jnp.arange(c, dtype=F32)
    diff = idx[:, None] - idx[None, :]
    kscale = RET_DK ** -0.5
    amat = jnp.where(diff >= 0, jnp.exp(jnp.maximum(diff, 0.0)[None] * lgf),
                     jnp.exp(jnp.maximum(-diff, 0.0)[None] * lgb)) * kscale
    col = idx[None, :, None]
    ones_k = jnp.ones((1, 1, RET_DK), F32)
    ones_v = jnp.ones((1, 1, RET_DV), F32)
    return dict(
        amat=amat,
        zf=jnp.exp((c - 1.0 - col) * lgf) * kscale * ones_k,
        zb=jnp.exp(col * lgb) * kscale * ones_k,
        xif=jnp.exp((col + 1.0) * lgf) * ones_v,
        xib=jnp.exp((c - col) * lgb) * ones_v,
        decf=jnp.exp(c * lgf) * ones_v,
        decb=jnp.exp(c * lgb) * ones_v,
    )


def _state_update(k_h, z_h, v_h):
    kz_t = jnp.transpose(k_h.astype(F32) * z_h).astype(BF16)
    return _dot(kz_t, v_h)


def _group_norm_gate(o, g_h):
    mu = jnp.mean(o, axis=-1, keepdims=True)
    d = o - mu
    var = jnp.mean(d * d, axis=-1, keepdims=True)
    g = g_h.astype(F32)
    return (_silu(g) * (d * lax.rsqrt(var + EPS))).astype(BF16)


def _ret_ctx_kernel(q_ref, k_ref, v_ref, g_ref, a_ref, zf_ref, zb_ref, y_ref, sf_ref, sb_ref):
    for h in range(RET_HEADS):
        q_h = q_ref[:, h * RET_DK:(h + 1) * RET_DK]
        k_h = k_ref[:, h * RET_DK:(h + 1) * RET_DK]
        v_h = v_ref[:, h * RET_DV:(h + 1) * RET_DV]
        inner = (_dot_nt(q_h, k_h) * a_ref[h]).astype(BF16)
        o = _dot(inner, v_h)
        y_ref[:, h * RET_DV:(h + 1) * RET_DV] = _group_norm_gate(o, g_ref[:, h * RET_DV:(h + 1) * RET_DV])
        sf_ref[h] = _state_update(k_h, zf_ref[h], v_h)
        sb_ref[h] = _state_update(k_h, zb_ref[h], v_h)


def _ret_ctx(proj, tabs, batch):
    c = RET_C
    qw, vw = RET_HEADS * RET_DK, RET_HEADS * RET_DV
    tab_spec = lambda arr: pl.BlockSpec(arr.shape, lambda b: (0, 0, 0))
    st_shape = jax.ShapeDtypeStruct((batch, RET_HEADS, RET_DK, RET_DV), F32)
    st_spec = pl.BlockSpec((None, RET_HEADS, RET_DK, RET_DV), lambda b: (b, 0, 0, 0))
    return pl.pallas_call(
        _ret_ctx_kernel,
        grid=(batch,),
        in_specs=[pl.BlockSpec((c, qw), lambda b: (b, 0)),
                  pl.BlockSpec((c, qw), lambda b: (b, 1)),
                  pl.BlockSpec((c, vw), lambda b: (b, 1)),
                  pl.BlockSpec((c, vw), lambda b: (b, 2)),
                  tab_spec(tabs["amat"]), tab_spec(tabs["zf"]), tab_spec(tabs["zb"])],
        out_specs=[pl.BlockSpec((c, vw), lambda b: (b, 0)), st_spec, st_spec],
        out_shape=[jax.ShapeDtypeStruct((batch * c, vw), BF16), st_shape, st_shape],
        compiler_params=_cparams(1),
        name="retention_context",
    )(proj, proj, proj, proj, tabs["amat"], tabs["zf"], tabs["zb"])


def _ret_bwd_kernel(q_ref, k_ref, v_ref, s0_ref, xib_ref, zb_ref, decb_ref, ob_ref, s_ref):
    @pl.when(pl.program_id(1) == 0)
    def _():
        s_ref[...] = s0_ref[...]

    for h in range(RET_HEADS):
        q_h = q_ref[:, h * RET_DK:(h + 1) * RET_DK]
        k_h = k_ref[:, h * RET_DK:(h + 1) * RET_DK]
        v_h = v_ref[:, h * RET_DV:(h + 1) * RET_DV]
        s = s_ref[h]
        ob_ref[:, h * RET_DV:(h + 1) * RET_DV] = _dot(q_h, s.astype(BF16)) * xib_ref[h]
        s_ref[h] = s * decb_ref[h] + _state_update(k_h, zb_ref[h], v_h)


def _ret_fwd_kernel(q_ref, k_ref, v_ref, g_ref, ob_ref, s0_ref, a_ref, xif_ref, zf_ref, decf_ref, y_ref, s_ref):
    @pl.when(pl.program_id(1) == 0)
    def _():
        s_ref[...] = s0_ref[...]

    for h in range(RET_HEADS):
        q_h = q_ref[:, h * RET_DK:(h + 1) * RET_DK]
        k_h = k_ref[:, h * RET_DK:(h + 1) * RET_DK]
        v_h = v_ref[:, h * RET_DV:(h + 1) * RET_DV]
        s = s_ref[h]
        inner = (_dot_nt(q_h, k_h) * a_ref[h]).astype(BF16)
        o = (_dot(inner, v_h) + _dot(q_h, s.astype(BF16)) * xif_ref[h]) + ob_ref[:, h * RET_DV:(h + 1) * RET_DV]
        y_ref[:, h * RET_DV:(h + 1) * RET_DV] = _group_norm_gate(o, g_ref[:, h * RET_DV:(h + 1) * RET_DV])
        s_ref[h] = s * decf_ref[h] + _state_update(k_h, zf_ref[h], v_h)


def _ret_lat(proj, tabs, s0_f, s0_b, batch, length):
    c = RET_C
    nc = length // c
    qw, vw = RET_HEADS * RET_DK, RET_HEADS * RET_DV
    t = batch * length
    tab_spec = lambda arr: pl.BlockSpec(arr.shape, lambda b, j: (0, 0, 0))
    st_spec = pl.BlockSpec((None, RET_HEADS, RET_DK, RET_DV), lambda b, j: (b, 0, 0, 0))
    st_scratch = pltpu.VMEM((RET_HEADS, RET_DK, RET_DV), F32)
    rev = lambda b, j: b * nc + (nc - 1 - j)
    fwd = lambda b, j: b * nc + j
    ob = pl.pallas_call(
        _ret_bwd_kernel,
        grid=(batch, nc),
        in_specs=[pl.BlockSpec((c, qw), lambda b, j: (rev(b, j), 0)),
                  pl.BlockSpec((c, qw), lambda b, j: (rev(b, j), 1)),
                  pl.BlockSpec((c, vw), lambda b, j: (rev(b, j), 1)),
                  st_spec, tab_spec(tabs["xib"]), tab_spec(tabs["zb"]), tab_spec(tabs["decb"])],
        out_specs=pl.BlockSpec((c, vw), lambda b, j: (rev(b, j), 0)),
        out_shape=jax.ShapeDtypeStruct((t, vw), F32),
        scratch_shapes=[st_scratch],
        compiler_params=_cparams(2),
        name="retention_latent_backward",
    )(proj, proj, proj, s0_b, tabs["xib"], tabs["zb"], tabs["decb"])
    return pl.pallas_call(
        _ret_fwd_kernel,
        grid=(batch, nc),
        in_specs=[pl.BlockSpec((c, qw), lambda b, j: (fwd(b, j), 0)),
                  pl.BlockSpec((c, qw), lambda b, j: (fwd(b, j), 1)),
                  pl.BlockSpec((c, vw), lambda b, j: (fwd(b, j), 1)),
                  pl.BlockSpec((c, vw), lambda b, j: (fwd(b, j), 2)),
                  pl.BlockSpec((c, vw), lambda b, j: (fwd(b, j), 0)),
                  st_spec, tab_spec(tabs["amat"]), tab_spec(tabs["xif"]), tab_spec(tabs["zf"]),
                  tab_spec(tabs["decf"])],
        out_specs=pl.BlockSpec((c, vw), lambda b, j: (fwd(b, j), 0)),
        out_shape=jax.ShapeDtypeStruct((t, vw), BF16),
        scratch_shapes=[st_scratch],
        compiler_params=_cparams(2),
        name="retention_latent_forward",
    )(proj, proj, proj, proj, ob, s0_f, tabs["amat"], tabs["xif"], tabs["zf"], tabs["decf"])


def _attn_ctx_kernel(sink_ref, q_ref, k_ref, v_ref, o_ref):
    scale = HEAD_DIM ** -0.5
    outs = []
    for g in range(ATTN_KV):
        k_g = k_ref[:, g * HEAD_DIM:(g + 1) * HEAD_DIM].astype(BF16)
        v_g = v_ref[:, g * HEAD_DIM:(g + 1) * HEAD_DIM].astype(BF16)
        for r in range(ATTN_REP):
            hh = g * ATTN_REP + r
            q_h = q_ref[:, hh * HEAD_DIM:(hh + 1) * HEAD_DIM].astype(BF16)
            s = _dot_nt(q_h, k_g) * scale
            snk = sink_ref[hh]
            m = jnp.maximum(jnp.max(s, axis=-1, keepdims=True), snk)
            p = jnp.exp(s - m)
            den = jnp.sum(p, axis=-1, keepdims=True) + jnp.exp(snk - m)
            outs.append(_dot(p.astype(BF16), v_g) / den)
    o_ref[...] = jnp.concatenate(outs, axis=1).astype(o_ref.dtype)


def _attn_ctx(qkv, sink, batch, length):
    qw = ATTN_HEADS * HEAD_DIM
    kw = ATTN_KV * HEAD_DIM
    return pl.pallas_call(
        _attn_ctx_kernel,
        grid=(batch,),
        in_specs=[pl.BlockSpec(memory_space=pltpu.SMEM),
                  pl.BlockSpec((length, qw), lambda b: (b, 0)),
                  pl.BlockSpec((length, kw), lambda b: (b, qw // kw)),
                  pl.BlockSpec((length, kw), lambda b: (b, qw // kw + 1))],
        out_specs=pl.BlockSpec((length, qw), lambda b: (b, 0)),
        out_shape=jax.ShapeDtypeStruct((batch * length, qw), BF16),
        compiler_params=_cparams(1),
        name="attention_context",
    )(sink, qkv, qkv, qkv)


def _attn_lat_kernel(sink_ref, q_ref, k_ref, v_ref, kc_ref, vc_ref, o_ref, *, length):
    scale = HEAD_DIM ** -0.5
    band = 3 * QBLK
    j = pl.program_id(1)
    start = pl.multiple_of(jnp.clip((j - 1) * QBLK, 0, length - band), QBLK)
    rows = ATTN_REP * QBLK
    qpos = j * QBLK + lax.broadcasted_iota(I32, (rows, band), 0) % QBLK
    kpos = start + lax.broadcasted_iota(I32, (rows, band), 1)
    valid = jnp.abs(qpos - kpos) <= WINDOW
    rep_of_row = lax.broadcasted_iota(I32, (rows, 1), 0) // QBLK
    kb = k_ref[pl.ds(start, band), :]
    vb = v_ref[pl.ds(start, band), :]
    outs = []
    for g in range(ATTN_KV):
        cs = slice(g * HEAD_DIM, (g + 1) * HEAD_DIM)
        k_g = kb[:, cs].astype(BF16)
        v_g = vb[:, cs].astype(BF16)
        kc_g = kc_ref[:, cs].astype(BF16)
        vc_g = vc_ref[:, cs].astype(BF16)
        q_g = jnp.concatenate(
            [q_ref[:, (g * ATTN_REP + r) * HEAD_DIM:(g * ATTN_REP + r + 1) * HEAD_DIM] for r in range(ATTN_REP)],
            axis=0).astype(BF16)
        snk = jnp.zeros((rows, 1), F32)
        for r in range(ATTN_REP):
            snk = jnp.where(rep_of_row == r, sink_ref[g * ATTN_REP + r], snk)
        s_loc = jnp.where(valid, _dot_nt(q_g, k_g) * scale, NEG_INF)
        s_ctx = _dot_nt(q_g, kc_g) * scale
        m = jnp.maximum(jnp.maximum(jnp.max(s_loc, axis=-1, keepdims=True),
                                    jnp.max(s_ctx, axis=-1, keepdims=True)), snk)
        p_loc = jnp.exp(s_loc - m)
        p_ctx = jnp.exp(s_ctx - m)
        den = (jnp.sum(p_loc, axis=-1, keepdims=True) + jnp.sum(p_ctx, axis=-1, keepdims=True)) + jnp.exp(snk - m)
        o_g = (_dot(p_loc.astype(BF16), v_g) + _dot(p_ctx.astype(BF16), vc_g)) / den
        for r in range(ATTN_REP):
            outs.append(o_g[r * QBLK:(r + 1) * QBLK, :])
    o_ref[...] = jnp.concatenate(outs, axis=1).astype(o_ref.dtype)


def _attn_lat(qkv, sink, kc, vc, batch, length):
    qw = ATTN_HEADS * HEAD_DIM
    kw = ATTN_KV * HEAD_DIM
    nq = length // QBLK
    past = kc.shape[1]
    return pl.pallas_call(
        functools.partial(_attn_lat_kernel, length=length),
        grid=(batch, nq),
        in_specs=[pl.BlockSpec(memory_space=pltpu.SMEM),
                  pl.BlockSpec((QBLK, qw), lambda b, j: (b * nq + j, 0)),
                  pl.BlockSpec((length, kw), lambda b, j: (b, qw // kw)),
                  pl.BlockSpec((length, kw), lambda b, j: (b, qw // kw + 1)),
                  pl.BlockSpec((None, past, kw), lambda b, j: (b, 0, 0)),
                  pl.BlockSpec((None, past, kw), lambda b, j: (b, 0, 0))],
        out_specs=pl.BlockSpec((QBLK, qw), lambda b, j: (b * nq + j, 0)),
        out_shape=jax.ShapeDtypeStruct((batch * length, qw), BF16),
        compiler_params=_cparams(2),
        name="attention_latent",
    )(sink, qkv, qkv, qkv, kc, vc)


def _rope_tables(length):
    axis_dim = HEAD_DIM // 2
    pos = jnp.arange(length)
    row = (pos // GRID_W).astype(F32)
    colp = (pos % GRID_W).astype(F32)
    inv = ROPE_BASE ** (-jnp.arange(0, axis_dim, 2, dtype=F32) / axis_dim)
    ar = row[:, None] * inv[None, :]
    ac = colp[:, None] * inv[None, :]
    cos = jnp.concatenate([jnp.cos(ar), jnp.cos(ar), jnp.cos(ac), jnp.cos(ac)], axis=1)
    sin = jnp.concatenate([-jnp.sin(ar), jnp.sin(ar), -jnp.sin(ac), jnp.sin(ac)], axis=1)
    return jnp.concatenate([cos, cos], axis=1), jnp.concatenate([sin, sin], axis=1)


def _router_kernel(x_ref, g_ref, sh_ref, sc_ref, wr_ref, br_ref, h_ref, e_ref, p1_ref, p2_ref):
    h = _norm_mod(x_ref[...], g_ref[...], sh_ref[...], sc_ref[...])
    _to_slabs(h_ref, h)
    logits = _dot3(h, wr_ref[...]) + br_ref[...]
    lane = lax.broadcasted_iota(I32, logits.shape, 1).astype(F32)
    ninf = -jnp.inf
    lg = jnp.where(lane < N_GROUPS, logits, ninf)
    gmax = jnp.max(lg, axis=-1, keepdims=True)
    gtop = jnp.min(jnp.where(lg == gmax, lane, float(LANE)), axis=-1, keepdims=True)
    pg_sel = 1.0 / jnp.sum(jnp.exp(lg - gmax), axis=-1, keepdims=True)
    lo = N_GROUPS + EXP_PER_GROUP * gtop
    le = jnp.where((lane >= lo) & (lane < lo + EXP_PER_GROUP), logits, ninf)
    v1 = jnp.max(le, axis=-1, keepdims=True)
    i1 = jnp.min(jnp.where(le == v1, lane, float(LANE)), axis=-1, keepdims=True)
    le2 = jnp.where(lane == i1, ninf, le)
    v2 = jnp.max(le2, axis=-1, keepdims=True)
    i2 = jnp.min(jnp.where(le2 == v2, lane, float(LANE)), axis=-1, keepdims=True)
    t = jnp.exp(v2 - v1)
    gate1 = pg_sel * (1.0 / (1.0 + t))
    gate2 = pg_sel * (t / (1.0 + t))
    e_ref[...] = jnp.where(lane == 0, i1 - N_GROUPS, jnp.where(lane == 1, i2 - N_GROUPS, 0.0)).astype(I32)
    p1_ref[...] = jnp.broadcast_to(gate1, p1_ref.shape)
    p2_ref[...] = jnp.broadcast_to(gate2, p2_ref.shape)


def _router(x, mods, layer, norm_g, wr, br, is_lat):
    t = x.shape[0]
    tm = 256
    seg_fn = _seg_of(is_lat, tm)
    return pl.pallas_call(
        _router_kernel,
        grid=(t // tm,),
        in_specs=[pl.BlockSpec((tm, D), lambda i: (i, 0)),
                  pl.BlockSpec((1, D), lambda i: (0, 0)),
                  _mod_spec(layer, 3, seg_fn),
                  _mod_spec(layer, 4, seg_fn),
                  pl.BlockSpec((D, LANE), lambda i: (0, 0)),
                  pl.BlockSpec((1, LANE), lambda i: (0, 0))],
        out_specs=[pl.BlockSpec((tm * SUB, LANE), lambda i: (i, 0)),
                   pl.BlockSpec((tm, LANE), lambda i: (i, 0)),
                   pl.BlockSpec((tm, LANE), lambda i: (i, 0)),
                   pl.BlockSpec((tm, LANE), lambda i: (i, 0))],
        out_shape=[jax.ShapeDtypeStruct((t * SUB, LANE), F32),
                   jax.ShapeDtypeStruct((t, LANE), I32),
                   jax.ShapeDtypeStruct((t, LANE), F32),
                   jax.ShapeDtypeStruct((t, LANE), F32)],
        compiler_params=_cparams(1),
        name="moe_router",
    )(x, norm_g.reshape(1, D), mods, mods, wr, br)


def _moe_kernel(dest_ref, meta_ref, h_hbm, g1_ref, g2_ref, w1_ref, w3_ref, w2_ref, y_hbm,
                h_vm, y_vm, xt, ot, gt, w1b, w3b, w2b, slot_tab, sem, *, tc, a_pad):
    chunk = pl.program_id(0)
    e = pl.program_id(1)
    stride = MOE_TILE_STRIDE
    n_lane_chunks = D // LANE
    h_rows = tc * SUB
    n_assign = 2 * tc

    def h_copy():
        return pltpu.make_async_copy(h_hbm.at[pl.ds(chunk * h_rows, h_rows)], h_vm.at[pl.ds(0, h_rows)], sem.at[0])

    def y_copy():
        return pltpu.make_async_copy(y_vm.at[pl.ds(0, h_rows)], y_hbm.at[pl.ds(chunk * h_rows, h_rows)], sem.at[1])

    @pl.when(e == 0)
    def _():
        h_copy().start()

        def zero(i, carry):
            y_vm[pl.ds(pl.multiple_of(i * 512, 512), 512), :] = jnp.zeros((512, LANE), F32)
            return carry

        lax.fori_loop(0, h_rows // 512, zero, 0)
        y_vm[pl.ds(h_rows, SUB), :] = jnp.zeros((SUB, LANE), F32)
        h_vm[pl.ds(h_rows, SUB), :] = jnp.zeros((SUB, LANE), F32)

        def fill(i, carry):
            for u in range(SUB):
                slot_tab[i * SUB + u] = n_assign
            return carry

        lax.fori_loop(0, a_pad // SUB, fill, 0)

        def invert(i, carry):
            for u in range(SUB):
                a_ = i * SUB + u
                slot_tab[dest_ref[chunk * n_assign + a_]] = a_
            return carry

        lax.fori_loop(0, n_assign // SUB, invert, 0)
        h_copy().wait()

    w1b[...] = w1_ref[...].astype(BF16)
    w3b[...] = w3_ref[...].astype(BF16)
    w2b[...] = w2_ref[...].astype(BF16)

    start = meta_ref[chunk * 2 * N_EXP + e]
    n_pad = meta_ref[chunk * 2 * N_EXP + N_EXP + e]

    def process(off, m):
        off = pl.multiple_of(off, MOE_PAD)

        def gather(i, carry):
            for u in range(SUB):
                r = i * SUB + u
                a_ = slot_tab[off + r]
                tok = lax.shift_right_logical(a_, 1)
                t8 = pl.multiple_of(tok * SUB, SUB)
                xt[pl.ds(r, SUB, stride=stride), :] = h_vm[pl.ds(t8, SUB), :]
                tg = jnp.minimum(tok, tc - 1)
                gt[pl.ds(r, 1), :] = jnp.where((a_ & 1) == 1, g2_ref[pl.ds(tg, 1), :], g1_ref[pl.ds(tg, 1), :])
            return carry

        lax.fori_loop(0, m // SUB, gather, 0)
        x = jnp.concatenate([xt[pl.ds(c * stride, m), :] for c in range(n_lane_chunks)], axis=1).astype(BF16)
        a = _dot(x, w1b[...])
        b = _dot(x, w3b[...])
        hm = (_silu(a) * b).astype(BF16)
        out = _dot(hm, w2b[...])
        gate = gt[pl.ds(0, m), :]
        for c in range(n_lane_chunks):
            ot[pl.ds(c * stride, m), :] = out[:, c * LANE:(c + 1) * LANE] * gate

        def scatter(i, carry):
            t8s, news = [], []
            for u in range(4):
                r = i * 4 + u
                t8 = pl.multiple_of(lax.shift_right_logical(slot_tab[off + r], 1) * SUB, SUB)
                t8s.append(t8)
                news.append(y_vm[pl.ds(t8, SUB), :] + ot[pl.ds(r, SUB, stride=stride), :])
            for u in range(4):
                y_vm[pl.ds(t8s[u], SUB), :] = news[u]
            return carry

        lax.fori_loop(0, m // 4, scatter, 0)

    n_full = n_pad // 256

    def full_body(i, carry):
        process(start + i * 256, 256)
        return carry

    lax.fori_loop(0, n_full, full_body, 0)
    rem = start + n_full * 256

    @pl.when((n_pad & 128) != 0)
    def _():
        process(rem, 128)

    @pl.when((n_pad & 64) != 0)
    def _():
        process(rem + (n_pad & 128), 64)

    @pl.when(e == N_EXP - 1)
    def _():
        y_copy().start()
        y_copy().wait()


def _moe_plan(experts):
    a = experts.shape[0] * 2
    e = experts.reshape(a)
    onehot = (e[:, None] == jnp.arange(N_EXP, dtype=I32)[None, :]).astype(I32)
    csum = jnp.cumsum(onehot, axis=0)
    counts = csum[-1]
    padded = (counts + MOE_PAD - 1) // MOE_PAD * MOE_PAD
    pad_start = jnp.cumsum(padded) - padded
    dest = jnp.sum(onehot * (csum - 1 + pad_start[None, :]), axis=1)
    return dest.astype(I32), jnp.concatenate([pad_start, padded]).astype(I32)


def _moe(h_slabs, experts, gate1, gate2, w1, w3, w2, layer):
    t = experts.shape[0]
    tc = SEG
    n_chunks = t // tc
    a_pad = 2 * tc + N_EXP * MOE_PAD
    dest, meta = jax.vmap(_moe_plan)(experts.reshape(n_chunks, tc, 2))
    slab_rows = (tc + 1) * SUB
    tile_rows = (D // LANE) * MOE_TILE_STRIDE
    gate_spec = pl.BlockSpec((tc, LANE), lambda c, e, *_: (c, 0), pipeline_mode=pl.Buffered(1))
    grid_spec = pltpu.PrefetchScalarGridSpec(
        num_scalar_prefetch=2,
        grid=(n_chunks, N_EXP),
        in_specs=[pl.BlockSpec(memory_space=pl.ANY), gate_spec, gate_spec,
                  pl.BlockSpec((None, None, D, D_FF), lambda c, e, *_: (layer, e, 0, 0)),
                  pl.BlockSpec((None, None, D, D_FF), lambda c, e, *_: (layer, e, 0, 0)),
                  pl.BlockSpec((None, None, D_FF, D), lambda c, e, *_: (layer, e, 0, 0))],
        out_specs=pl.BlockSpec(memory_space=pl.ANY),
        scratch_shapes=[pltpu.VMEM((slab_rows, LANE), F32),
                        pltpu.VMEM((slab_rows, LANE), F32),
                        pltpu.VMEM((tile_rows, LANE), F32),
                        pltpu.VMEM((tile_rows, LANE), F32),
                        pltpu.VMEM((256, LANE), F32),
                        pltpu.VMEM((D, D_FF), BF16),
                        pltpu.VMEM((D, D_FF), BF16),
                        pltpu.VMEM((D_FF, D), BF16),
                        pltpu.SMEM((a_pad,), I32),
                        pltpu.SemaphoreType.DMA((2,))],
    )
    return pl.pallas_call(
        functools.partial(_moe_kernel, tc=tc, a_pad=a_pad),
        grid_spec=grid_spec,
        out_shape=jax.ShapeDtypeStruct((t * SUB, LANE), F32),
        compiler_params=pltpu.CompilerParams(dimension_semantics=("arbitrary", "arbitrary"),
                                             vmem_limit_bytes=V7X_VMEM_LIMIT_MOE),
        name="moe_experts",
    )(dest.reshape(n_chunks * 2 * tc), meta.reshape(n_chunks * 2 * N_EXP), h_slabs, gate1, gate2, w1, w3, w2)


def kernel(x_prompt, x_sample, c, cache_attn_k, cache_attn_v, state_ret_fwd, state_ret_bwd, c_ctx, adaln_w, adaln_b, norm1_g, norm2_g, final_g, fnet_w, ret_w_in, ret_w_out, ret_decay_fwd, ret_decay_bwd, attn_w_in, attn_w_out, attn_sink, router_group_w, router_group_b, router_expert_w, router_expert_b, moe_w1, moe_w3, moe_w2):
    bp, lp, _ = x_prompt.shape
    bs, ls, _ = x_sample.shape
    xs = {False: x_prompt.reshape(bp * lp, D), True: x_sample.reshape(bs * ls, D)}
    dims = {False: (bp, lp), True: (bs, ls)}

    cond8 = jnp.concatenate([c_ctx[None, :], c, jnp.zeros((SUB - 1 - bs, D), F32)], axis=0)
    mods = _adaln(cond8, adaln_w, adaln_b).reshape(DEPTH, SUB, 6, 1, D)

    rope_lat = _rope_tables(ls)
    rope = {False: (jnp.ones((bp * lp, LANE), F32), jnp.zeros((bp * lp, LANE), F32)),
            True: (jnp.tile(rope_lat[0], (bs, 1)), jnp.tile(rope_lat[1], (bs, 1)))}

    extras = {}
    for i in range(DEPTH):
        kind, j = i % 3, i // 3
        if kind == 0:
            lw = dict(w_fold=_fnet_fold(fnet_w[j]))
        elif kind == 1:
            lw = dict(w_in=ret_w_in[j].astype(BF16), w_out=ret_w_out[j].astype(BF16),
                      tabs=_ret_tables(ret_decay_fwd[j], ret_decay_bwd[j]),
                      s0_f=state_ret_fwd[:, j], s0_b=state_ret_bwd[:, j])
        else:
            kvw = ATTN_KV * HEAD_DIM
            lw = dict(w_in=attn_w_in[j].astype(BF16), w_out=attn_w_out[j].astype(BF16), sink=attn_sink[j],
                      kc=cache_attn_k[:, j].reshape(bs, -1, kvw), vc=cache_attn_v[:, j].reshape(bs, -1, kvw))
        wr = jnp.concatenate([router_group_w[i], jnp.transpose(router_expert_w[i], (1, 0, 2)).reshape(D, N_EXP),
                              jnp.zeros((D, LANE - N_GROUPS - N_EXP), F32)], axis=1)
        br = jnp.concatenate([router_group_b[i], router_expert_b[i].reshape(N_EXP),
                              jnp.zeros((LANE - N_GROUPS - N_EXP,), F32)]).reshape(1, LANE)

        for is_lat in (False, True):
            b_, l_ = dims[is_lat]
            x = _mixer(i, xs[is_lat], mods, norm1_g[i], lw, is_lat, b_, l_, rope[is_lat], extras)
            xs[is_lat] = _ffn(i, x, mods, norm2_g[i], wr, br, moe_w1, moe_w3, moe_w2, is_lat)

    y_prompt = _final_norm(xs[False], final_g).reshape(bp, lp, D)
    y_sample = _final_norm(xs[True], final_g).reshape(bs, ls, D)
    return (y_prompt, y_sample, extras["k"], extras["v"], extras["sf"], extras["sb"])


def _mixer(i, x, mods, norm_g, lw, is_lat, b_, l_, rope, extras):
    kind = i % 3
    if kind == 0:
        y = _proj(x, mods, i, 0, norm_g, lw["w_fold"], is_lat, BF16)
        return _seqmix(y, x, mods, i, is_lat, b_, l_)
    if kind == 1:
        proj = _proj(x, mods, i, 0, norm_g, lw["w_in"], is_lat, BF16)
        if is_lat:
            yr = _ret_lat(proj, lw["tabs"], lw["s0_f"], lw["s0_b"], b_, l_)
        else:
            yr, sf, sb = _ret_ctx(proj, lw["tabs"], b_)
            extras["sf"], extras["sb"] = sf[:, None], sb[:, None]
        return _linres(yr, lw["w_out"], x, mods, i, 2, is_lat)
    qkv = _proj(x, mods, i, 0, norm_g, lw["w_in"], is_lat, F32, rope=rope)
    if is_lat:
        o = _attn_lat(qkv, lw["sink"], lw["kc"], lw["vc"], b_, l_)
    else:
        qw = ATTN_HEADS * HEAD_DIM
        kw = ATTN_KV * HEAD_DIM
        extras["k"] = qkv[:, qw:qw + kw].reshape(b_, 1, l_, ATTN_KV, HEAD_DIM)
        extras["v"] = qkv[:, qw + kw:].reshape(b_, 1, l_, ATTN_KV, HEAD_DIM)
        o = _attn_ctx(qkv, lw["sink"], b_, l_)
    return _linres(o, lw["w_out"], x, mods, i, 2, is_lat)


def _ffn(i, x, mods, norm_g, wr, br, w1, w3, w2, is_lat):
    h_slabs, e_out, gate1, gate2 = _router(x, mods, i, norm_g, wr, br, is_lat)
    y_slabs = _moe(h_slabs, e_out[:, :2], gate1, gate2, w1, w3, w2, i)
    return _residual(x, y_slabs, mods, i, 5, is_lat)
```

```python
import functools
import math

import jax
import jax.numpy as jnp
from jax import lax
from jax.experimental import pallas as pl
from jax.experimental.pallas import tpu as pltpu

F32 = jnp.float32
BF16 = jnp.bfloat16
I32 = jnp.int32

D = 1024
DEPTH = 4
SEG = 4096
FNET_GROUP_DIM = 256
RET_HEADS, RET_DK, RET_DV = 4, 256, 512
RET_C = 256
ATTN_HEADS, ATTN_KV, HEAD_DIM, ATTN_REP = 16, 4, 64, 4
WINDOW, QBLK = 128, 128
GRID_W = 64
ROPE_BASE = 10000.0
N_EXP, EXP_PER_GROUP, N_GROUPS, D_FF = 32, 8, 4, 512
EPS = 1e-6
NEG_INF = -1e30
LANE = 128
SUB = 8

V7X_VMEM_LIMIT = 56 * 1024 * 1024
V7X_VMEM_LIMIT_MOE = 59 * 1024 * 1024
MOE_PAD = 64
MOE_TILE_STRIDE = 264


def _cparams(n_axes):
    return pltpu.CompilerParams(dimension_semantics=("arbitrary",) * n_axes, vmem_limit_bytes=V7X_VMEM_LIMIT)


def _dot(a, b):
    return jnp.dot(a, b, preferred_element_type=F32)


def _dot_nt(a, b):
    return lax.dot_general(a, b, (((1,), (1,)), ((), ())), preferred_element_type=F32)


def _split(x):
    hi = x.astype(BF16)
    lo = (x - hi.astype(F32)).astype(BF16)
    return hi, lo


def _dot3(a, b):
    ah, al = _split(a)
    bh, bl = _split(b)
    return _dot(ah, bh) + (_dot(ah, bl) + _dot(al, bh))


def _silu(x):
    return x * jax.nn.sigmoid(x)


def _norm_mod(x, g, sh, sc):
    ms = jnp.mean(x * x, axis=-1, keepdims=True)
    return (x * lax.rsqrt(ms + EPS) * g) * (1.0 + sc) + sh


def _seg_of(is_lat, tm):
    if is_lat:
        return lambda i: 1 + (i * tm) // SEG
    return lambda i: 0


def _mod_spec(layer, kind, seg_fn):
    return pl.BlockSpec((None, None, None, 1, D), lambda i, *_: (layer, seg_fn(i), kind, 0, 0))


def _adaln_kernel(c_ref, w_ref, b_ref, o_ref):
    o_ref[...] = _dot3(_silu(c_ref[...]), w_ref[...]) + b_ref[...]


def _adaln(cond8, w, b):
    depth, _, n = w.shape
    tn = 1536
    return pl.pallas_call(
        _adaln_kernel,
        grid=(depth, n // tn),
        in_specs=[pl.BlockSpec((SUB, D), lambda l, j: (0, 0)),
                  pl.BlockSpec((None, D, tn), lambda l, j: (l, 0, j)),
                  pl.BlockSpec((None, 1, tn), lambda l, j: (l, 0, j))],
        out_specs=pl.BlockSpec((None, SUB, tn), lambda l, j: (l, 0, j)),
        out_shape=jax.ShapeDtypeStruct((depth, SUB, n), F32),
        compiler_params=_cparams(2),
        name="adaln",
    )(cond8, w, b.reshape(depth, 1, n))


def _rope(x, cos_t, sin_t):
    w = x.shape[1]
    reps = w // LANE
    c = jnp.concatenate([cos_t] * reps, axis=1)
    s = jnp.concatenate([sin_t] * reps, axis=1)
    lane = lax.broadcasted_iota(I32, x.shape, 1)
    first = (lane % 32) < 16
    partner = jnp.where(first, pltpu.roll(x, w - 16, 1), pltpu.roll(x, 16, 1))
    return x * c + partner * s


def _proj_kernel(x_ref, g_ref, sh_ref, sc_ref, w_ref, *rest, n_rope):
    if n_rope:
        cos_ref, sin_ref, o_ref = rest
    else:
        (o_ref,) = rest
    hb = _norm_mod(x_ref[...], g_ref[...], sh_ref[...], sc_ref[...]).astype(BF16)
    n = w_ref.shape[1]
    tn = 512
    if n_rope:
        q_w = ATTN_HEADS * HEAD_DIM
        k_w = ATTN_KV * HEAD_DIM
        q = _dot(hb, w_ref[:, :q_w])
        o_ref[:, :q_w] = _rope(q, cos_ref[...], sin_ref[...])
        k = _dot(hb, w_ref[:, q_w:q_w + k_w])
        o_ref[:, q_w:q_w + k_w] = _rope(k, cos_ref[...], sin_ref[...])
        o_ref[:, q_w + k_w:] = _dot(hb, w_ref[:, q_w + k_w:])
    else:
        for j in range(n // tn):
            o_ref[:, j * tn:(j + 1) * tn] = _dot(hb, w_ref[:, j * tn:(j + 1) * tn]).astype(o_ref.dtype)


def _proj(x, mods, layer, sub, norm_g, w_bf16, is_lat, out_dtype, rope=None):
    t = x.shape[0]
    n = w_bf16.shape[1]
    tm = 256
    seg_fn = _seg_of(is_lat, tm)
    in_specs = [pl.BlockSpec((tm, D), lambda i: (i, 0)),
                pl.BlockSpec((1, D), lambda i: (0, 0)),
                _mod_spec(layer, 3 * sub + 0, seg_fn),
                _mod_spec(layer, 3 * sub + 1, seg_fn),
                pl.BlockSpec((D, n), lambda i: (0, 0))]
    args = [x, norm_g.reshape(1, D), mods, mods, w_bf16]
    if rope is not None:
        in_specs += [pl.BlockSpec((tm, LANE), lambda i: (i, 0))] * 2
        args += list(rope)
    return pl.pallas_call(
        functools.partial(_proj_kernel, n_rope=rope is not None),
        grid=(t // tm,),
        in_specs=in_specs,
        out_specs=pl.BlockSpec((tm, n), lambda i: (i, 0)),
        out_shape=jax.ShapeDtypeStruct((t, n), out_dtype),
        compiler_params=_cparams(1),
        name="norm_mod_proj",
    )(*args)


def _linres_kernel(y_ref, w_ref, x_ref, g_ref, o_ref):
    o_ref[...] = x_ref[...] + g_ref[...] * _dot(y_ref[...].astype(BF16), w_ref[...])


def _linres(y, w_bf16, x, mods, layer, kind, is_lat):
    t, k = y.shape
    tm = 256
    seg_fn = _seg_of(is_lat, tm)
    return pl.pallas_call(
        _linres_kernel,
        grid=(t // tm,),
        in_specs=[pl.BlockSpec((tm, k), lambda i: (i, 0)),
                  pl.BlockSpec((k, D), lambda i: (0, 0)),
                  pl.BlockSpec((tm, D), lambda i: (i, 0)),
                  _mod_spec(layer, kind, seg_fn)],
        out_specs=pl.BlockSpec((tm, D), lambda i: (i, 0)),
        out_shape=jax.ShapeDtypeStruct((t, D), F32),
        compiler_params=_cparams(1),
        name="linear_residual",
    )(y, w_bf16, x, mods)


def _to_slabs(slab_ref, x):
    tm = x.shape[0]
    for c in range(D // LANE):
        slab_ref[pl.ds(c, tm, stride=SUB), :] = x[:, c * LANE:(c + 1) * LANE]


def _from_slabs(slab_ref, tm):
    return jnp.concatenate([slab_ref[pl.ds(c, tm, stride=SUB), :] for c in range(D // LANE)], axis=1)


def _residual_kernel(x_ref, y_ref, g_ref, o_ref):
    o_ref[...] = x_ref[...] + g_ref[...] * _from_slabs(y_ref, x_ref.shape[0])


def _residual(x, y_slabs, mods, layer, kind, is_lat):
    t = x.shape[0]
    tm = 512
    seg_fn = _seg_of(is_lat, tm)
    return pl.pallas_call(
        _residual_kernel,
        grid=(t // tm,),
        in_specs=[pl.BlockSpec((tm, D), lambda i: (i, 0)),
                  pl.BlockSpec((tm * SUB, LANE), lambda i: (i, 0)),
                  _mod_spec(layer, kind, seg_fn)],
        out_specs=pl.BlockSpec((tm, D), lambda i: (i, 0)),
        out_shape=jax.ShapeDtypeStruct((t, D), F32),
        compiler_params=_cparams(1),
        name="gated_residual",
    )(x, y_slabs, mods)


def _final_norm_kernel(x_ref, g_ref, o_ref):
    x = x_ref[...]
    ms = jnp.mean(x * x, axis=-1, keepdims=True)
    o_ref[...] = x * lax.rsqrt(ms + EPS) * g_ref[...]


def _final_norm(x, g):
    t = x.shape[0]
    tm = 512
    return pl.pallas_call(
        _final_norm_kernel,
        grid=(t // tm,),
        in_specs=[pl.BlockSpec((tm, D), lambda i: (i, 0)), pl.BlockSpec((1, D), lambda i: (0, 0))],
        out_specs=pl.BlockSpec((tm, D), lambda i: (i, 0)),
        out_shape=jax.ShapeDtypeStruct((t, D), F32),
        compiler_params=_cparams(1),
        name="final_rmsnorm",
    )(x, g.reshape(1, D))


def _fold_kernel(c_ref, s_ref, w_ref, o_ref):
    w = w_ref[...]
    o_ref[:, :D] = _dot3(c_ref[...], w).astype(BF16)
    o_ref[:, D:] = _dot3(s_ref[...], w).astype(BF16)


def _fnet_fold(w):
    n = jnp.arange(FNET_GROUP_DIM, dtype=I32)
    ang = ((n[:, None] * n[None, :]) % FNET_GROUP_DIM).astype(F32) * (2.0 * math.pi / FNET_GROUP_DIM)
    sc = 1.0 / math.sqrt(FNET_GROUP_DIM)
    gd = FNET_GROUP_DIM
    return pl.pallas_call(
        _fold_kernel,
        grid=(D // gd,),
        in_specs=[pl.BlockSpec((gd, gd), lambda g: (0, 0)),
                  pl.BlockSpec((gd, gd), lambda g: (0, 0)),
                  pl.BlockSpec((gd, D), lambda g: (g, 0))],
        out_specs=pl.BlockSpec((gd, 2 * D), lambda g: (g, 0)),
        out_shape=jax.ShapeDtypeStruct((D, 2 * D), BF16),
        compiler_params=_cparams(1),
        name="fnet_fold_channel_dft",
    )(jnp.cos(ang) * sc, jnp.sin(ang) * sc, w)


def _seq_dft(length):
    n = jnp.arange(length, dtype=I32)
    sc = 1.0 / math.sqrt(length)
    split = 64
    if length <= split * 4:
        ang = ((n[:, None] * n[None, :]) % length).astype(F32) * (2.0 * math.pi / length)
        return (jnp.cos(ang) * sc).astype(BF16), (jnp.sin(ang) * (-sc)).astype(BF16)
    hi = jnp.arange(length // split, dtype=I32)
    lo = jnp.arange(split, dtype=I32)
    ang_a = ((n[:, None] * hi[None, :]) % (length // split)).astype(F32) * (2.0 * math.pi * split / length)
    ang_b = ((n[:, None] * lo[None, :]) % length).astype(F32) * (2.0 * math.pi / length)
    ca, sa = jnp.cos(ang_a)[:, :, None], jnp.sin(ang_a)[:, :, None]
    cb, sb = (jnp.cos(ang_b) * sc)[:, None, :], (jnp.sin(ang_b) * sc)[:, None, :]
    cm = (ca * cb - sa * sb).reshape(length, length)
    sm = (sa * cb + ca * sb).reshape(length, length)
    return cm.astype(BF16), (-sm).astype(BF16)


def _seqmix_kernel(c_ref, s_ref, y_ref, x_ref, g_ref, o_ref, acc_ref, *, nk):
    k = pl.program_id(2)

    @pl.when(k == 0)
    def _():
        acc_ref[...] = jnp.zeros_like(acc_ref)

    acc_ref[...] += _dot(c_ref[...], y_ref[:, :D]) + _dot(s_ref[...], y_ref[:, D:])

    @pl.when(k == nk - 1)
    def _():
        o_ref[...] = x_ref[...] + g_ref[...] * acc_ref[...]


def _seqmix(y, x, mods, layer, is_lat, batch, length):
    cm, sm = _seq_dft(length)
    tm = min(length, 1024)
    tk = min(length, 1024)
    nm, nk = length // tm, length // tk
    seg = (lambda b: 1 + b) if is_lat else (lambda b: 0)
    out = pl.pallas_call(
        functools.partial(_seqmix_kernel, nk=nk),
        grid=(batch, nm, nk),
        in_specs=[pl.BlockSpec((tm, tk), lambda b, m, k: (m, k)),
                  pl.BlockSpec((tm, tk), lambda b, m, k: (m, k)),
                  pl.BlockSpec((None, tk, 2 * D), lambda b, m, k: (b, k, 0)),
                  pl.BlockSpec((None, tm, D), lambda b, m, k: (b, m, 0)),
                  pl.BlockSpec((None, None, None, 1, D), lambda b, m, k: (layer, seg(b), 2, 0, 0))],
        out_specs=pl.BlockSpec((None, tm, D), lambda b, m, k: (b, m, 0)),
        out_shape=jax.ShapeDtypeStruct((batch, length, D), F32),
        scratch_shapes=[pltpu.VMEM((tm, D), F32)],
        compiler_params=_cparams(3),
        name="fnet_seq_dft",
    )(cm, sm, y.reshape(batch, length, 2 * D), x.reshape(batch, length, D), mods)
    return out.reshape(batch * length, D)


def _ret_tables(dec_f, dec_b):
    c = RET_C
    lgf = jax.nn.log_sigmoid(dec_f.astype(F32))[:, None, None]
    lgb = jax.nn.log_sigmoid(dec_b.astype(F32))[:, None, None]
    idx = jnp.arange(c, dtype=F32)
    diff = idx[:, None] - idx[None, :]
    kscale = RET_DK ** -0.5
    amat = jnp.where(diff >= 0, jnp.exp(jnp.maximum(diff, 0.0)[None] * lgf),
                     jnp.exp(jnp.maximum(-diff, 0.0)[None] * lgb)) * kscale
    col = idx[None, :, None]
    ones_k = jnp.ones((1, 1, RET_DK), F32)
    ones_v = jnp.ones((1, 1, RET_DV), F32)
    return dict(
        amat=amat,
        zf=jnp.exp((c - 1.0 - col) * lgf) * kscale * ones_k,
        zb=jnp.exp(col * lgb) * kscale * ones_k,
        xif=jnp.exp((col + 1.0) * lgf) * ones_v,
        xib=jnp.exp((c - col) * lgb) * ones_v,
        decf=jnp.exp(c * lgf) * ones_v,
        decb=jnp.exp(c * lgb) * ones_v,
    )


def _state_update(k_h, z_h, v_h):
    kz_t = jnp.transpose(k_h.astype(F32) * z_h).astype(BF16)
    return _dot(kz_t, v_h)


def _group_norm_gate(o, g_h):
    mu = jnp.mean(o, axis=-1, keepdims=True)
    d = o - mu
    var = jnp.mean(d * d, axis=-1, keepdims=True)
    g = g_h.astype(F32)
    return (_silu(g) * (d * lax.rsqrt(var + EPS))).astype(BF16)


def _ret_ctx_kernel(q_ref, k_ref, v_ref, g_ref, a_ref, zf_ref, zb_ref, y_ref, sf_ref, sb_ref):
    for h in range(RET_HEADS):
        q_h = q_ref[:, h * RET_DK:(h + 1) * RET_DK]
        k_h = k_ref[:, h * RET_DK:(h + 1) * RET_DK]
        v_h = v_ref[:, h * RET_DV:(h + 1) * RET_DV]
        inner = (_dot_nt(q_h, k_h) * a_ref[h]).astype(BF16)
        o = _dot(inner, v_h)
        y_ref[:, h * RET_DV:(h + 1) * RET_DV] = _group_norm_gate(o, g_ref[:, h * RET_DV:(h + 1) * RET_DV])
        sf_ref[h] = _state_update(k_h, zf_ref[h], v_h)
        sb_ref[h] = _state_update(k_h, zb_ref[h], v_h)


def _ret_ctx(proj, tabs, batch):
    c = RET_C
    qw, vw = RET_HEADS * RET_DK, RET_HEADS * RET_DV
    tab_spec = lambda arr: pl.BlockSpec(arr.shape, lambda b: (0, 0, 0))
    st_shape = jax.ShapeDtypeStruct((batch, RET_HEADS, RET_DK, RET_DV), F32)
    st_spec = pl.BlockSpec((None, RET_HEADS, RET_DK, RET_DV), lambda b: (b, 0, 0, 0))
    return pl.pallas_call(
        _ret_ctx_kernel,
        grid=(batch,),
        in_specs=[pl.BlockSpec((c, qw), lambda b: (b, 0)),
                  pl.BlockSpec((c, qw), lambda b: (b, 1)),
                  pl.BlockSpec((c, vw), lambda b: (b, 1)),
                  pl.BlockSpec((c, vw), lambda b: (b, 2)),
                  tab_spec(tabs["amat"]), tab_spec(tabs["zf"]), tab_spec(tabs["zb"])],
        out_specs=[pl.BlockSpec((c, vw), lambda b: (b, 0)), st_spec, st_spec],
        out_shape=[jax.ShapeDtypeStruct((batch * c, vw), BF16), st_shape, st_shape],
        compiler_params=_cparams(1),
        name="retention_context",
    )(proj, proj, proj, proj, tabs["amat"], tabs["zf"], tabs["zb"])


def _ret_bwd_kernel(q_ref, k_ref, v_ref, s0_ref, xib_ref, zb_ref, decb_ref, ob_ref, s_ref):
    @pl.when(pl.program_id(1) == 0)
    def _():
        s_ref[...] = s0_ref[...]

    for h in range(RET_HEADS):
        q_h = q_ref[:, h * RET_DK:(h + 1) * RET_DK]
        k_h = k_ref[:, h * RET_DK:(h + 1) * RET_DK]
        v_h = v_ref[:, h * RET_DV:(h + 1) * RET_DV]
        s = s_ref[h]
        ob_ref[:, h * RET_DV:(h + 1) * RET_DV] = _dot(q_h, s.astype(BF16)) * xib_ref[h]
        s_ref[h] = s * decb_ref[h] + _state_update(k_h, zb_ref[h], v_h)


def _ret_fwd_kernel(q_ref, k_ref, v_ref, g_ref, ob_ref, s0_ref, a_ref, xif_ref, zf_ref, decf_ref, y_ref, s_ref):
    @pl.when(pl.program_id(1) == 0)
    def _():
        s_ref[...] = s0_ref[...]

    for h in range(RET_HEADS):
        q_h = q_ref[:, h * RET_DK:(h + 1) * RET_DK]
        k_h = k_ref[:, h * RET_DK:(h + 1) * RET_DK]
        v_h = v_ref[:, h * RET_DV:(h + 1) * RET_DV]
        s = s_ref[h]
        inner = (_dot_nt(q_h, k_h) * a_ref[h]).astype(BF16)
        o = (_dot(inner, v_h) + _dot(q_h, s.astype(BF16)) * xif_ref[h]) + ob_ref[:, h * RET_DV:(h + 1) * RET_DV]
        y_ref[:, h * RET_DV:(h + 1) * RET_DV] = _group_norm_gate(o, g_ref[:, h * RET_DV:(h + 1) * RET_DV])
        s_ref[h] = s * decf_ref[h] + _state_update(k_h, zf_ref[h], v_h)


def _ret_lat(proj, tabs, s0_f, s0_b, batch, length):
    c = RET_C
    nc = length // c
    qw, vw = RET_HEADS * RET_DK, RET_HEADS * RET_DV
    t = batch * length
    tab_spec = lambda arr: pl.BlockSpec(arr.shape, lambda b, j: (0, 0, 0))
    st_spec = pl.BlockSpec((None, RET_HEADS, RET_DK, RET_DV), lambda b, j: (b, 0, 0, 0))
    st_scratch = pltpu.VMEM((RET_HEADS, RET_DK, RET_DV), F32)
    rev = lambda b, j: b * nc + (nc - 1 - j)
    fwd = lambda b, j: b * nc + j
    ob = pl.pallas_call(
        _ret_bwd_kernel,
        grid=(batch, nc),
        in_specs=[pl.BlockSpec((c, qw), lambda b, j: (rev(b, j), 0)),
                  pl.BlockSpec((c, qw), lambda b, j: (rev(b, j), 1)),
                  pl.BlockSpec((c, vw), lambda b, j: (rev(b, j), 1)),
                  st_spec, tab_spec(tabs["xib"]), tab_spec(tabs["zb"]), tab_spec(tabs["decb"])],
        out_specs=pl.BlockSpec((c, vw), lambda b, j: (rev(b, j), 0)),
        out_shape=jax.ShapeDtypeStruct((t, vw), F32),
        scratch_shapes=[st_scratch],
        compiler_params=_cparams(2),
        name="retention_latent_backward",
    )(proj, proj, proj, s0_b, tabs["xib"], tabs["zb"], tabs["decb"])
    return pl.pallas_call(
        _ret_fwd_kernel,
        grid=(batch, nc),
        in_specs=[pl.BlockSpec((c, qw), lambda b, j: (fwd(b, j), 0)),
                  pl.BlockSpec((c, qw), lambda b, j: (fwd(b, j), 1)),
                  pl.BlockSpec((c, vw), lambda b, j: (fwd(b, j), 1)),
                  pl.BlockSpec((c, vw), lambda b, j: (fwd(b, j), 2)),
                  pl.BlockSpec((c, vw), lambda b, j: (fwd(b, j), 0)),
                  st_spec, tab_spec(tabs["amat"]), tab_spec(tabs["xif"]), tab_spec(tabs["zf"]),
                  tab_spec(tabs["decf"])],
        out_specs=pl.BlockSpec((c, vw), lambda b, j: (fwd(b, j), 0)),
        out_shape=jax.ShapeDtypeStruct((t, vw), BF16),
        scratch_shapes=[st_scratch],
        compiler_params=_cparams(2),
        name="retention_latent_forward",
    )(proj, proj, proj, proj, ob, s0_f, tabs["amat"], tabs["xif"], tabs["zf"], tabs["decf"])


def _attn_ctx_kernel(sink_ref, q_ref, k_ref, v_ref, o_ref):
    scale = HEAD_DIM ** -0.5
    outs = []
    for g in range(ATTN_KV):
        k_g = k_ref[:, g * HEAD_DIM:(g + 1) * HEAD_DIM].astype(BF16)
        v_g = v_ref[:, g * HEAD_DIM:(g + 1) * HEAD_DIM].astype(BF16)
        for r in range(ATTN_REP):
            hh = g * ATTN_REP + r
            q_h = q_ref[:, hh * HEAD_DIM:(hh + 1) * HEAD_DIM].astype(BF16)
            s = _dot_nt(q_h, k_g) * scale
            snk = sink_ref[hh]
            m = jnp.maximum(jnp.max(s, axis=-1, keepdims=True), snk)
            p = jnp.exp(s - m)
            den = jnp.sum(p, axis=-1, keepdims=True) + jnp.exp(snk - m)
            outs.append(_dot(p.astype(BF16), v_g) / den)
    o_ref[...] = jnp.concatenate(outs, axis=1).astype(o_ref.dtype)


def _attn_ctx(qkv, sink, batch, length):
    qw = ATTN_HEADS * HEAD_DIM
    kw = ATTN_KV * HEAD_DIM
    return pl.pallas_call(
        _attn_ctx_kernel,
        grid=(batch,),
        in_specs=[pl.BlockSpec(memory_space=pltpu.SMEM),
                  pl.BlockSpec((length, qw), lambda b: (b, 0)),
                  pl.BlockSpec((length, kw), lambda b: (b, qw // kw)),
                  pl.BlockSpec((length, kw), lambda b: (b, qw // kw + 1))],
        out_specs=pl.BlockSpec((length, qw), lambda b: (b, 0)),
        out_shape=jax.ShapeDtypeStruct((batch * length, qw), BF16),
        compiler_params=_cparams(1),
        name="attention_context",
    )(sink, qkv, qkv, qkv)


def _attn_lat_kernel(sink_ref, q_ref, k_ref, v_ref, kc_ref, vc_ref, o_ref, *, length):
    scale = HEAD_DIM ** -0.5
    band = 3 * QBLK
    j = pl.program_id(1)
    start = pl.multiple_of(jnp.clip((j - 1) * QBLK, 0, length - band), QBLK)
    rows = ATTN_REP * QBLK
    qpos = j * QBLK + lax.broadcasted_iota(I32, (rows, band), 0) % QBLK
    kpos = start + lax.broadcasted_iota(I32, (rows, band), 1)
    valid = jnp.abs(qpos - kpos) <= WINDOW
    rep_of_row = lax.broadcasted_iota(I32, (rows, 1), 0) // QBLK
    kb = k_ref[pl.ds(start, band), :]
    vb = v_ref[pl.ds(start, band), :]
    outs = []
    for g in range(ATTN_KV):
        cs = slice(g * HEAD_DIM, (g + 1) * HEAD_DIM)
        k_g = kb[:, cs].astype(BF16)
        v_g = vb[:, cs].astype(BF16)
        kc_g = kc_ref[:, cs].astype(BF16)
        vc_g = vc_ref[:, cs].astype(BF16)
        q_g = jnp.concatenate(
            [q_ref[:, (g * ATTN_REP + r) * HEAD_DIM:(g * ATTN_REP + r + 1) * HEAD_DIM] for r in range(ATTN_REP)],
            axis=0).astype(BF16)
        snk = jnp.zeros((rows, 1), F32)
        for r in range(ATTN_REP):
            snk = jnp.where(rep_of_row == r, sink_ref[g * ATTN_REP + r], snk)
        s_loc = jnp.where(valid, _dot_nt(q_g, k_g) * scale, NEG_INF)
        s_ctx = _dot_nt(q_g, kc_g) * scale
        m = jnp.maximum(jnp.maximum(jnp.max(s_loc, axis=-1, keepdims=True),
                                    jnp.max(s_ctx, axis=-1, keepdims=True)), snk)
        p_loc = jnp.exp(s_loc - m)
        p_ctx = jnp.exp(s_ctx - m)
        den = (jnp.sum(p_loc, axis=-1, keepdims=True) + jnp.sum(p_ctx, axis=-1, keepdims=True)) + jnp.exp(snk - m)
        o_g = (_dot(p_loc.astype(BF16), v_g) + _dot(p_ctx.astype(BF16), vc_g)) / den
        for r in range(ATTN_REP):
            outs.append(o_g[r * QBLK:(r + 1) * QBLK, :])
    o_ref[...] = jnp.concatenate(outs, axis=1).astype(o_ref.dtype)


def _attn_lat(qkv, sink, kc, vc, batch, length):
    qw = ATTN_HEADS * HEAD_DIM
    kw = ATTN_KV * HEAD_DIM
    nq = length // QBLK
    past = kc.shape[1]
    return pl.pallas_call(
        functools.partial(_attn_lat_kernel, length=length),
        grid=(batch, nq),
        in_specs=[pl.BlockSpec(memory_space=pltpu.SMEM),
                  pl.BlockSpec((QBLK, qw), lambda b, j: (b * nq + j, 0)),
                  pl.BlockSpec((length, kw), lambda b, j: (b, qw // kw)),
                  pl.BlockSpec((length, kw), lambda b, j: (b, qw // kw + 1)),
                  pl.BlockSpec((None, past, kw), lambda b, j: (b, 0, 0)),
                  pl.BlockSpec((None, past, kw), lambda b, j: (b, 0, 0))],
        out_specs=pl.BlockSpec((QBLK, qw), lambda b, j: (b * nq + j, 0)),
        out_shape=jax.ShapeDtypeStruct((batch * length, qw), BF16),
        compiler_params=_cparams(2),
        name="attention_latent",
    )(sink, qkv, qkv, qkv, kc, vc)


def _rope_tables(length):
    axis_dim = HEAD_DIM // 2
    pos = jnp.arange(length)
    row = (pos // GRID_W).astype(F32)
    colp = (pos % GRID_W).astype(F32)
    inv = ROPE_BASE ** (-jnp.arange(0, axis_dim, 2, dtype=F32) / axis_dim)
    ar = row[:, None] * inv[None, :]
    ac = colp[:, None] * inv[None, :]
    cos = jnp.concatenate([jnp.cos(ar), jnp.cos(ar), jnp.cos(ac), jnp.cos(ac)], axis=1)
    sin = jnp.concatenate([-jnp.sin(ar), jnp.sin(ar), -jnp.sin(ac), jnp.sin(ac)], axis=1)
    return jnp.concatenate([cos, cos], axis=1), jnp.concatenate([sin, sin], axis=1)


def _router_kernel(x_ref, g_ref, sh_ref, sc_ref, wr_ref, br_ref, h_ref, e_ref, p1_ref, p2_ref, n_ref, cnt_ref,
                   *, tiles_per_chunk):
    h = _norm_mod(x_ref[...], g_ref[...], sh_ref[...], sc_ref[...])
    _to_slabs(h_ref, h)
    logits = _dot3(h, wr_ref[...]) + br_ref[...]
    lane = lax.broadcasted_iota(I32, logits.shape, 1).astype(F32)
    ninf = -jnp.inf
    lg = jnp.where(lane < N_GROUPS, logits, ninf)
    gmax = jnp.max(lg, axis=-1, keepdims=True)
    gtop = jnp.min(jnp.where(lg == gmax, lane, float(LANE)), axis=-1, keepdims=True)
    pg_sel = 1.0 / jnp.sum(jnp.exp(lg - gmax), axis=-1, keepdims=True)
    lo = N_GROUPS + EXP_PER_GROUP * gtop
    le = jnp.where((lane >= lo) & (lane < lo + EXP_PER_GROUP), logits, ninf)
    v1 = jnp.max(le, axis=-1, keepdims=True)
    i1 = jnp.min(jnp.where(le == v1, lane, float(LANE)), axis=-1, keepdims=True)
    le2 = jnp.where(lane == i1, ninf, le)
    v2 = jnp.max(le2, axis=-1, keepdims=True)
    i2 = jnp.min(jnp.where(le2 == v2, lane, float(LANE)), axis=-1, keepdims=True)
    t = jnp.exp(v2 - v1)
    gate1 = pg_sel * (1.0 / (1.0 + t))
    gate2 = pg_sel * (t / (1.0 + t))
    p1_ref[...] = jnp.broadcast_to(gate1, p1_ref.shape)
    p2_ref[...] = jnp.broadcast_to(gate2, p2_ref.shape)

    e1 = i1 - N_GROUPS
    e2 = i2 - N_GROUPS
    tm = logits.shape[0]

    @pl.when(pl.program_id(0) % tiles_per_chunk == 0)
    def _():
        cnt_ref[...] = jnp.zeros_like(cnt_ref)

    carry = cnt_ref[0:1, :]
    oh1 = (lane == e1).astype(F32)
    oh2 = (lane == e2).astype(F32)
    tri = (lax.broadcasted_iota(I32, (tm, tm), 0) >= lax.broadcasted_iota(I32, (tm, tm), 1)).astype(BF16)
    cs1 = _dot(tri, oh1.astype(BF16))
    cs2 = _dot(tri, oh2.astype(BF16))
    tot1 = cs1[tm - 1:tm, :]
    tot2 = cs2[tm - 1:tm, :]
    rank1 = jnp.sum(oh1 * (carry + cs1 - 1.0), axis=-1, keepdims=True)
    rank2 = jnp.sum(oh2 * ((carry + tot1) + cs2 - 1.0), axis=-1, keepdims=True)
    code1 = rank1 * float(N_EXP) + e1
    code2 = rank2 * float(N_EXP) + e2
    e_ref[...] = jnp.where(lane == 0, code1, jnp.where(lane == 1, code2, 0.0)).astype(I32)
    new_cnt = (carry + tot1) + tot2
    cnt_ref[...] = jnp.broadcast_to(new_cnt, cnt_ref.shape)
    n_ref[...] = jnp.broadcast_to(new_cnt, n_ref.shape).astype(I32)


def _router(x, mods, layer, norm_g, wr, br, is_lat):
    t = x.shape[0]
    tm = 256
    seg_fn = _seg_of(is_lat, tm)
    tiles_per_chunk = SEG // tm
    return pl.pallas_call(
        functools.partial(_router_kernel, tiles_per_chunk=tiles_per_chunk),
        grid=(t // tm,),
        in_specs=[pl.BlockSpec((tm, D), lambda i: (i, 0)),
                  pl.BlockSpec((1, D), lambda i: (0, 0)),
                  _mod_spec(layer, 3, seg_fn),
                  _mod_spec(layer, 4, seg_fn),
                  pl.BlockSpec((D, LANE), lambda i: (0, 0)),
                  pl.BlockSpec((1, LANE), lambda i: (0, 0))],
        out_specs=[pl.BlockSpec((tm * SUB, LANE), lambda i: (i, 0)),
                   pl.BlockSpec((tm, LANE), lambda i: (i, 0)),
                   pl.BlockSpec((tm, LANE), lambda i: (i, 0)),
                   pl.BlockSpec((tm, LANE), lambda i: (i, 0)),
                   pl.BlockSpec((SUB, LANE), lambda i: (i // tiles_per_chunk, 0))],
        out_shape=[jax.ShapeDtypeStruct((t * SUB, LANE), F32),
                   jax.ShapeDtypeStruct((t, LANE), I32),
                   jax.ShapeDtypeStruct((t, LANE), F32),
                   jax.ShapeDtypeStruct((t, LANE), F32),
                   jax.ShapeDtypeStruct((t // SEG * SUB, LANE), I32)],
        scratch_shapes=[pltpu.VMEM((SUB, LANE), F32)],
        compiler_params=_cparams(1),
        name="moe_router",
    )(x, norm_g.reshape(1, D), mods, mods, wr, br)


def _moe_kernel(dest_ref, meta_ref, h_hbm, g1_ref, g2_ref, w1_ref, w3_ref, w2_ref, y_hbm,
                h_vm, y_vm, xt, ot, gt, w1b, w3b, w2b, slot_tab, sem, *, tc, a_pad):
    chunk = pl.program_id(0)
    e = pl.program_id(1)
    stride = MOE_TILE_STRIDE
    n_lane_chunks = D // LANE
    h_rows = tc * SUB
    n_assign = 2 * tc

    def h_copy():
        return pltpu.make_async_copy(h_hbm.at[pl.ds(chunk * h_rows, h_rows)], h_vm.at[pl.ds(0, h_rows)], sem.at[0])

    def y_copy():
        return pltpu.make_async_copy(y_vm.at[pl.ds(0, h_rows)], y_hbm.at[pl.ds(chunk * h_rows, h_rows)], sem.at[1])

    @pl.when(e == 0)
    def _():
        h_copy().start()

        def zero(i, carry):
            y_vm[pl.ds(pl.multiple_of(i * 512, 512), 512), :] = jnp.zeros((512, LANE), F32)
            return carry

        lax.fori_loop(0, h_rows // 512, zero, 0)
        y_vm[pl.ds(h_rows, SUB), :] = jnp.zeros((SUB, LANE), F32)
        h_vm[pl.ds(h_rows, SUB), :] = jnp.zeros((SUB, LANE), F32)

        def fill(i, carry):
            for u in range(SUB):
                slot_tab[i * SUB + u] = n_assign
            return carry

        lax.fori_loop(0, a_pad // SUB, fill, 0)

        def invert(i, carry):
            for u in range(SUB):
                a_ = i * SUB + u
                code = dest_ref[chunk * n_assign + a_]
                slot = meta_ref[chunk * 2 * N_EXP + (code & (N_EXP - 1))] + lax.shift_right_logical(code, 5)
                slot_tab[slot] = a_
            return carry

        lax.fori_loop(0, n_assign // SUB, invert, 0)
        h_copy().wait()

    w1b[...] = w1_ref[...].astype(BF16)
    w3b[...] = w3_ref[...].astype(BF16)
    w2b[...] = w2_ref[...].astype(BF16)

    start = meta_ref[chunk * 2 * N_EXP + e]
    n_pad = meta_ref[chunk * 2 * N_EXP + N_EXP + e]

    def process(off, m):
        off = pl.multiple_of(off, MOE_PAD)

        def gather(i, carry):
            for u in range(SUB):
                r = i * SUB + u
                a_ = slot_tab[off + r]
                tok = lax.shift_right_logical(a_, 1)
                t8 = pl.multiple_of(tok * SUB, SUB)
                xt[pl.ds(r, SUB, stride=stride), :] = h_vm[pl.ds(t8, SUB), :]
                tg = jnp.minimum(tok, tc - 1)
                gt[pl.ds(r, 1), :] = jnp.where((a_ & 1) == 1, g2_ref[pl.ds(tg, 1), :], g1_ref[pl.ds(tg, 1), :])
            return carry

        lax.fori_loop(0, m // SUB, gather, 0)
        x = jnp.concatenate([xt[pl.ds(c * stride, m), :] for c in range(n_lane_chunks)], axis=1).astype(BF16)
        a = _dot(x, w1b[...])
        b = _dot(x, w3b[...])
        hm = (_silu(a) * b).astype(BF16)
        out = _dot(hm, w2b[...])
        gate = gt[pl.ds(0, m), :]
        for c in range(n_lane_chunks):
            ot[pl.ds(c * stride, m), :] = out[:, c * LANE:(c + 1) * LANE] * gate

        def scatter(i, carry):
            t8s, news = [], []
            for u in range(4):
                r = i * 4 + u
                t8 = pl.multiple_of(lax.shift_right_logical(slot_tab[off + r], 1) * SUB, SUB)
                t8s.append(t8)
                news.append(y_vm[pl.ds(t8, SUB), :] + ot[pl.ds(r, SUB, stride=stride), :])
            for u in range(4):
                y_vm[pl.ds(t8s[u], SUB), :] = news[u]
            return carry

        lax.fori_loop(0, m // 4, scatter, 0)

    n_full = n_pad // 256

    def full_body(i, carry):
        process(start + i * 256, 256)
        return carry

    lax.fori_loop(0, n_full, full_body, 0)
    rem = start + n_full * 256

    @pl.when((n_pad & 128) != 0)
    def _():
        process(rem, 128)

    @pl.when((n_pad & 64) != 0)
    def _():
        process(rem + (n_pad & 128), 64)

    @pl.when(e == N_EXP - 1)
    def _():
        y_copy().start()
        y_copy().wait()


def _moe(h_slabs, codes, counts, gate1, gate2, w1, w3, w2, layer):
    t = codes.shape[0]
    tc = SEG
    n_chunks = t // tc
    a_pad = 2 * tc + N_EXP * MOE_PAD
    padded = (counts + MOE_PAD - 1) // MOE_PAD * MOE_PAD
    meta = jnp.concatenate([jnp.cumsum(padded, axis=1) - padded, padded], axis=1).astype(I32)
    dest = codes
    slab_rows = (tc + 1) * SUB
    tile_rows = (D // LANE) * MOE_TILE_STRIDE
    gate_spec = pl.BlockSpec((tc, LANE), lambda c, e, *_: (c, 0), pipeline_mode=pl.Buffered(1))
    grid_spec = pltpu.PrefetchScalarGridSpec(
        num_scalar_prefetch=2,
        grid=(n_chunks, N_EXP),
        in_specs=[pl.BlockSpec(memory_space=pl.ANY), gate_spec, gate_spec,
                  pl.BlockSpec((None, None, D, D_FF), lambda c, e, *_: (layer, e, 0, 0)),
                  pl.BlockSpec((None, None, D, D_FF), lambda c, e, *_: (layer, e, 0, 0)),
                  pl.BlockSpec((None, None, D_FF, D), lambda c, e, *_: (layer, e, 0, 0))],
        out_specs=pl.BlockSpec(memory_space=pl.ANY),
        scratch_shapes=[pltpu.VMEM((slab_rows, LANE), F32),
                        pltpu.VMEM((slab_rows, LANE), F32),
                        pltpu.VMEM((tile_rows, LANE), F32),
                        pltpu.VMEM((tile_rows, LANE), F32),
                        pltpu.VMEM((256, LANE), F32),
                        pltpu.VMEM((D, D_FF), BF16),
                        pltpu.VMEM((D, D_FF), BF16),
                        pltpu.VMEM((D_FF, D), BF16),
                        pltpu.SMEM((a_pad,), I32),
                        pltpu.SemaphoreType.DMA((2,))],
    )
    return pl.pallas_call(
        functools.partial(_moe_kernel, tc=tc, a_pad=a_pad),
        grid_spec=grid_spec,
        out_shape=jax.ShapeDtypeStruct((t * SUB, LANE), F32),
        compiler_params=pltpu.CompilerParams(dimension_semantics=("arbitrary", "arbitrary"),
                                             vmem_limit_bytes=V7X_VMEM_LIMIT_MOE),
        name="moe_experts",
    )(dest.reshape(n_chunks * 2 * tc), meta.reshape(n_chunks * 2 * N_EXP), h_slabs, gate1, gate2, w1, w3, w2)


def kernel(x_prompt, x_sample, c, cache_attn_k, cache_attn_v, state_ret_fwd, state_ret_bwd, c_ctx, adaln_w, adaln_b, norm1_g, norm2_g, final_g, fnet_w, ret_w_in, ret_w_out, ret_decay_fwd, ret_decay_bwd, attn_w_in, attn_w_out, attn_sink, router_group_w, router_group_b, router_expert_w, router_expert_b, moe_w1, moe_w3, moe_w2):
    bp, lp, _ = x_prompt.shape
    bs, ls, _ = x_sample.shape
    xs = {False: x_prompt.reshape(bp * lp, D), True: x_sample.reshape(bs * ls, D)}
    dims = {False: (bp, lp), True: (bs, ls)}

    cond8 = jnp.concatenate([c_ctx[None, :], c, jnp.zeros((SUB - 1 - bs, D), F32)], axis=0)
    mods = _adaln(cond8, adaln_w, adaln_b).reshape(DEPTH, SUB, 6, 1, D)

    rope_lat = _rope_tables(ls)
    rope = {False: (jnp.ones((bp * lp, LANE), F32), jnp.zeros((bp * lp, LANE), F32)),
            True: (jnp.tile(rope_lat[0], (bs, 1)), jnp.tile(rope_lat[1], (bs, 1)))}

    extras = {}
    for i in range(DEPTH):
        kind, j = i % 3, i // 3
        if kind == 0:
            lw = dict(w_fold=_fnet_fold(fnet_w[j]))
        elif kind == 1:
            lw = dict(w_in=ret_w_in[j].astype(BF16), w_out=ret_w_out[j].astype(BF16),
                      tabs=_ret_tables(ret_decay_fwd[j], ret_decay_bwd[j]),
                      s0_f=state_ret_fwd[:, j], s0_b=state_ret_bwd[:, j])
        else:
            kvw = ATTN_KV * HEAD_DIM
            lw = dict(w_in=attn_w_in[j].astype(BF16), w_out=attn_w_out[j].astype(BF16), sink=attn_sink[j],
                      kc=cache_attn_k[:, j].reshape(bs, -1, kvw), vc=cache_attn_v[:, j].reshape(bs, -1, kvw))
        wr = jnp.concatenate([router_group_w[i], jnp.transpose(router_expert_w[i], (1, 0, 2)).reshape(D, N_EXP),
                              jnp.zeros((D, LANE - N_GROUPS - N_EXP), F32)], axis=1)
        br = jnp.concatenate([router_group_b[i], router_expert_b[i].reshape(N_EXP),
                              jnp.zeros((LANE - N_GROUPS - N_EXP,), F32)]).reshape(1, LANE)

        for is_lat in (False, True):
            b_, l_ = dims[is_lat]
            x = _mixer(i, xs[is_lat], mods, norm1_g[i], lw, is_lat, b_, l_, rope[is_lat], extras)
            xs[is_lat] = _ffn(i, x, mods, norm2_g[i], wr, br, moe_w1, moe_w3, moe_w2, is_lat)

    y_prompt = _final_norm(xs[False], final_g).reshape(bp, lp, D)
    y_sample = _final_norm(xs[True], final_g).reshape(bs, ls, D)
    return (y_prompt, y_sample, extras["k"], extras["v"], extras["sf"], extras["sb"])


def _mixer(i, x, mods, norm_g, lw, is_lat, b_, l_, rope, extras):
    kind = i % 3
    if kind == 0:
        y = _proj(x, mods, i, 0, norm_g, lw["w_fold"], is_lat, BF16)
        return _seqmix(y, x, mods, i, is_lat, b_, l_)
    if kind == 1:
        proj = _proj(x, mods, i, 0, norm_g, lw["w_in"], is_lat, BF16)
        if is_lat:
            yr = _ret_lat(proj, lw["tabs"], lw["s0_f"], lw["s0_b"], b_, l_)
        else:
            yr, sf, sb = _ret_ctx(proj, lw["tabs"], b_)
            extras["sf"], extras["sb"] = sf[:, None], sb[:, None]
        return _linres(yr, lw["w_out"], x, mods, i, 2, is_lat)
    qkv = _proj(x, mods, i, 0, norm_g, lw["w_in"], is_lat, F32, rope=rope)
    if is_lat:
        o = _attn_lat(qkv, lw["sink"], lw["kc"], lw["vc"], b_, l_)
    else:
        qw = ATTN_HEADS * HEAD_DIM
        kw = ATTN_KV * HEAD_DIM
        extras["k"] = qkv[:, qw:qw + kw].reshape(b_, 1, l_, ATTN_KV, HEAD_DIM)
        extras["v"] = qkv[:, qw + kw:].reshape(b_, 1, l_, ATTN_KV, HEAD_DIM)
        o = _attn_ctx(qkv, lw["sink"], b_, l_)
    return _linres(o, lw["w_out"], x, mods, i, 2, is_lat)


def _ffn(i, x, mods, norm_g, wr, br, w1, w3, w2, is_lat):
    h_slabs, codes, gate1, gate2, counts = _router(x, mods, i, norm_g, wr, br, is_lat)
    counts = counts.reshape(-1, SUB, LANE)[:, 0, :N_EXP]
    y_slabs = _moe(h_slabs, codes[:, :2], counts, gate1, gate2, w1, w3, w2, i)
    return _residual(x, y_slabs, mods, i, 5, is_lat)
```

```python
import functools
import math

import jax
import jax.numpy as jnp
from jax import lax
from jax.experimental import pallas as pl
from jax.experimental.pallas import tpu as pltpu

F32 = jnp.float32
BF16 = jnp.bfloat16
I32 = jnp.int32

D = 1024
DEPTH = 4
SEG = 4096
FNET_GROUP_DIM = 256
RET_HEADS, RET_DK, RET_DV = 4, 256, 512
RET_C = 256
ATTN_HEADS, ATTN_KV, HEAD_DIM, ATTN_REP = 16, 4, 64, 4
WINDOW, QBLK = 128, 128
GRID_W = 64
ROPE_BASE = 10000.0
N_EXP, EXP_PER_GROUP, N_GROUPS, D_FF = 32, 8, 4, 512
EPS = 1e-6
NEG_INF = -1e30
LANE = 128
SUB = 8

V7X_VMEM_LIMIT = 56 * 1024 * 1024
V7X_VMEM_LIMIT_MOE = 59 * 1024 * 1024
MOE_PAD = 64
MOE_TILE_STRIDE = 264


def _cparams(n_axes):
    return pltpu.CompilerParams(dimension_semantics=("arbitrary",) * n_axes, vmem_limit_bytes=V7X_VMEM_LIMIT)


def _dot(a, b):
    return jnp.dot(a, b, preferred_element_type=F32)


def _dot_nt(a, b):
    return lax.dot_general(a, b, (((1,), (1,)), ((), ())), preferred_element_type=F32)


def _split(x):
    hi = x.astype(BF16)
    lo = (x - hi.astype(F32)).astype(BF16)
    return hi, lo


def _dot3(a, b):
    ah, al = _split(a)
    bh, bl = _split(b)
    return _dot(ah, bh) + (_dot(ah, bl) + _dot(al, bh))


def _silu(x):
    return x * jax.nn.sigmoid(x)


def _norm_mod(x, g, sh, sc):
    ms = jnp.mean(x * x, axis=-1, keepdims=True)
    return (x * lax.rsqrt(ms + EPS) * g) * (1.0 + sc) + sh


def _seg_of(is_lat, tm):
    if is_lat:
        return lambda i: 1 + (i * tm) // SEG
    return lambda i: 0


def _mod_spec(layer, kind, seg_fn):
    return pl.BlockSpec((None, None, None, 1, D), lambda i, *_: (layer, seg_fn(i), kind, 0, 0))


def _adaln_kernel(c_ref, w_ref, b_ref, o_ref):
    o_ref[...] = _dot3(_silu(c_ref[...]), w_ref[...]) + b_ref[...]


def _adaln(cond8, w, b):
    depth, _, n = w.shape
    tn = 1536
    return pl.pallas_call(
        _adaln_kernel,
        grid=(depth, n // tn),
        in_specs=[pl.BlockSpec((SUB, D), lambda l, j: (0, 0)),
                  pl.BlockSpec((None, D, tn), lambda l, j: (l, 0, j)),
                  pl.BlockSpec((None, 1, tn), lambda l, j: (l, 0, j))],
        out_specs=pl.BlockSpec((None, SUB, tn), lambda l, j: (l, 0, j)),
        out_shape=jax.ShapeDtypeStruct((depth, SUB, n), F32),
        compiler_params=_cparams(2),
        name="adaln",
    )(cond8, w, b.reshape(depth, 1, n))


def _rope(x, cos_t, sin_t):
    w = x.shape[1]
    reps = w // LANE
    c = jnp.concatenate([cos_t] * reps, axis=1)
    s = jnp.concatenate([sin_t] * reps, axis=1)
    lane = lax.broadcasted_iota(I32, x.shape, 1)
    first = (lane % 32) < 16
    partner = jnp.where(first, pltpu.roll(x, w - 16, 1), pltpu.roll(x, 16, 1))
    return x * c + partner * s


def _proj_kernel(x_ref, g_ref, sh_ref, sc_ref, w_ref, *rest, n_rope):
    if n_rope:
        cos_ref, sin_ref, o_ref = rest
    else:
        (o_ref,) = rest
    hb = _norm_mod(x_ref[...], g_ref[...], sh_ref[...], sc_ref[...]).astype(BF16)
    n = w_ref.shape[1]
    tn = 512
    if n_rope:
        q_w = ATTN_HEADS * HEAD_DIM
        k_w = ATTN_KV * HEAD_DIM
        q = _dot(hb, w_ref[:, :q_w])
        o_ref[:, :q_w] = _rope(q, cos_ref[...], sin_ref[...])
        k = _dot(hb, w_ref[:, q_w:q_w + k_w])
        o_ref[:, q_w:q_w + k_w] = _rope(k, cos_ref[...], sin_ref[...])
        o_ref[:, q_w + k_w:] = _dot(hb, w_ref[:, q_w + k_w:])
    else:
        for j in range(n // tn):
            o_ref[:, j * tn:(j + 1) * tn] = _dot(hb, w_ref[:, j * tn:(j + 1) * tn]).astype(o_ref.dtype)


def _proj(x, mods, layer, sub, norm_g, w_bf16, is_lat, out_dtype, rope=None):
    t = x.shape[0]
    n = w_bf16.shape[1]
    tm = 256
    seg_fn = _seg_of(is_lat, tm)
    in_specs = [pl.BlockSpec((tm, D), lambda i: (i, 0)),
                pl.BlockSpec((1, D), lambda i: (0, 0)),
                _mod_spec(layer, 3 * sub + 0, seg_fn),
                _mod_spec(layer, 3 * sub + 1, seg_fn),
                pl.BlockSpec((D, n), lambda i: (0, 0))]
    args = [x, norm_g.reshape(1, D), mods, mods, w_bf16]
    if rope is not None:
        in_specs += [pl.BlockSpec((tm, LANE), lambda i: (i, 0))] * 2
        args += list(rope)
    return pl.pallas_call(
        functools.partial(_proj_kernel, n_rope=rope is not None),
        grid=(t // tm,),
        in_specs=in_specs,
        out_specs=pl.BlockSpec((tm, n), lambda i: (i, 0)),
        out_shape=jax.ShapeDtypeStruct((t, n), out_dtype),
        compiler_params=_cparams(1),
        name="norm_mod_proj",
    )(*args)


def _linres_kernel(y_ref, w_ref, x_ref, g_ref, o_ref):
    o_ref[...] = x_ref[...] + g_ref[...] * _dot(y_ref[...].astype(BF16), w_ref[...])


def _linres(y, w_bf16, x, mods, layer, kind, is_lat):
    t, k = y.shape
    tm = 256
    seg_fn = _seg_of(is_lat, tm)
    return pl.pallas_call(
        _linres_kernel,
        grid=(t // tm,),
        in_specs=[pl.BlockSpec((tm, k), lambda i: (i, 0)),
                  pl.BlockSpec((k, D), lambda i: (0, 0)),
                  pl.BlockSpec((tm, D), lambda i: (i, 0)),
                  _mod_spec(layer, kind, seg_fn)],
        out_specs=pl.BlockSpec((tm, D), lambda i: (i, 0)),
        out_shape=jax.ShapeDtypeStruct((t, D), F32),
        compiler_params=_cparams(1),
        name="linear_residual",
    )(y, w_bf16, x, mods)


def _to_slabs(slab_ref, x):
    tm = x.shape[0]
    for c in range(D // LANE):
        slab_ref[pl.ds(c, tm, stride=SUB), :] = x[:, c * LANE:(c + 1) * LANE]


def _from_slabs(slab_ref, tm):
    return jnp.concatenate([slab_ref[pl.ds(c, tm, stride=SUB), :] for c in range(D // LANE)], axis=1)


def _residual_kernel(x_ref, y_ref, g_ref, o_ref):
    o_ref[...] = x_ref[...] + g_ref[...] * _from_slabs(y_ref, x_ref.shape[0])


def _residual(x, y_slabs, mods, layer, kind, is_lat):
    t = x.shape[0]
    tm = 512
    seg_fn = _seg_of(is_lat, tm)
    return pl.pallas_call(
        _residual_kernel,
        grid=(t // tm,),
        in_specs=[pl.BlockSpec((tm, D), lambda i: (i, 0)),
                  pl.BlockSpec((tm * SUB, LANE), lambda i: (i, 0)),
                  _mod_spec(layer, kind, seg_fn)],
        out_specs=pl.BlockSpec((tm, D), lambda i: (i, 0)),
        out_shape=jax.ShapeDtypeStruct((t, D), F32),
        compiler_params=_cparams(1),
        name="gated_residual",
    )(x, y_slabs, mods)


def _final_norm_kernel(x_ref, g_ref, o_ref):
    x = x_ref[...]
    ms = jnp.mean(x * x, axis=-1, keepdims=True)
    o_ref[...] = x * lax.rsqrt(ms + EPS) * g_ref[...]


def _final_norm(x, g):
    t = x.shape[0]
    tm = 512
    return pl.pallas_call(
        _final_norm_kernel,
        grid=(t // tm,),
        in_specs=[pl.BlockSpec((tm, D), lambda i: (i, 0)), pl.BlockSpec((1, D), lambda i: (0, 0))],
        out_specs=pl.BlockSpec((tm, D), lambda i: (i, 0)),
        out_shape=jax.ShapeDtypeStruct((t, D), F32),
        compiler_params=_cparams(1),
        name="final_rmsnorm",
    )(x, g.reshape(1, D))


def _fold_kernel(c_ref, s_ref, w_ref, o_ref):
    w = w_ref[...]
    o_ref[:, :D] = _dot3(c_ref[...], w).astype(BF16)
    o_ref[:, D:] = _dot3(s_ref[...], w).astype(BF16)


def _fnet_fold(w):
    n = jnp.arange(FNET_GROUP_DIM, dtype=I32)
    ang = ((n[:, None] * n[None, :]) % FNET_GROUP_DIM).astype(F32) * (2.0 * math.pi / FNET_GROUP_DIM)
    sc = 1.0 / math.sqrt(FNET_GROUP_DIM)
    gd = FNET_GROUP_DIM
    return pl.pallas_call(
        _fold_kernel,
        grid=(D // gd,),
        in_specs=[pl.BlockSpec((gd, gd), lambda g: (0, 0)),
                  pl.BlockSpec((gd, gd), lambda g: (0, 0)),
                  pl.BlockSpec((gd, D), lambda g: (g, 0))],
        out_specs=pl.BlockSpec((gd, 2 * D), lambda g: (g, 0)),
        out_shape=jax.ShapeDtypeStruct((D, 2 * D), BF16),
        compiler_params=_cparams(1),
        name="fnet_fold_channel_dft",
    )(jnp.cos(ang) * sc, jnp.sin(ang) * sc, w)


def _seq_dft(length):
    n = jnp.arange(length, dtype=I32)
    sc = 1.0 / math.sqrt(length)
    split = 64
    if length <= split * 4:
        ang = ((n[:, None] * n[None, :]) % length).astype(F32) * (2.0 * math.pi / length)
        return (jnp.cos(ang) * sc).astype(BF16), (jnp.sin(ang) * (-sc)).astype(BF16)
    hi = jnp.arange(length // split, dtype=I32)
    lo = jnp.arange(split, dtype=I32)
    ang_a = ((n[:, None] * hi[None, :]) % (length // split)).astype(F32) * (2.0 * math.pi * split / length)
    ang_b = ((n[:, None] * lo[None, :]) % length).astype(F32) * (2.0 * math.pi / length)
    ca, sa = jnp.cos(ang_a)[:, :, None], jnp.sin(ang_a)[:, :, None]
    cb, sb = (jnp.cos(ang_b) * sc)[:, None, :], (jnp.sin(ang_b) * sc)[:, None, :]
    cm = (ca * cb - sa * sb).reshape(length, length)
    sm = (sa * cb + ca * sb).reshape(length, length)
    return cm.astype(BF16), (-sm).astype(BF16)


def _seqmix_kernel(c_ref, s_ref, y_ref, x_ref, g_ref, o_ref, acc_ref, *, nk):
    k = pl.program_id(2)

    @pl.when(k == 0)
    def _():
        acc_ref[...] = jnp.zeros_like(acc_ref)

    acc_ref[...] += _dot(c_ref[...], y_ref[:, :D]) + _dot(s_ref[...], y_ref[:, D:])

    @pl.when(k == nk - 1)
    def _():
        o_ref[...] = x_ref[...] + g_ref[...] * acc_ref[...]


def _seqmix(y, x, mods, layer, is_lat, batch, length):
    cm, sm = _seq_dft(length)
    tm = min(length, 1024)
    tk = min(length, 1024)
    nm, nk = length // tm, length // tk
    seg = (lambda b: 1 + b) if is_lat else (lambda b: 0)
    out = pl.pallas_call(
        functools.partial(_seqmix_kernel, nk=nk),
        grid=(batch, nm, nk),
        in_specs=[pl.BlockSpec((tm, tk), lambda b, m, k: (m, k)),
                  pl.BlockSpec((tm, tk), lambda b, m, k: (m, k)),
                  pl.BlockSpec((None, tk, 2 * D), lambda b, m, k: (b, k, 0)),
                  pl.BlockSpec((None, tm, D), lambda b, m, k: (b, m, 0)),
                  pl.BlockSpec((None, None, None, 1, D), lambda b, m, k: (layer, seg(b), 2, 0, 0))],
        out_specs=pl.BlockSpec((None, tm, D), lambda b, m, k: (b, m, 0)),
        out_shape=jax.ShapeDtypeStruct((batch, length, D), F32),
        scratch_shapes=[pltpu.VMEM((tm, D), F32)],
        compiler_params=_cparams(3),
        name="fnet_seq_dft",
    )(cm, sm, y.reshape(batch, length, 2 * D), x.reshape(batch, length, D), mods)
    return out.reshape(batch * length, D)


def _ret_tables(dec_f, dec_b):
    c = RET_C
    lgf = jax.nn.log_sigmoid(dec_f.astype(F32))[:, None, None]
    lgb = jax.nn.log_sigmoid(dec_b.astype(F32))[:, None, None]
    idx = jnp.arange(c, dtype=F32)
    diff = idx[:, None] - idx[None, :]
    kscale = RET_DK ** -0.5
    amat = jnp.where(diff >= 0, jnp.exp(jnp.maximum(diff, 0.0)[None] * lgf),
                     jnp.exp(jnp.maximum(-diff, 0.0)[None] * lgb)) * kscale
    col = idx[None, :, None]
    ones_k = jnp.ones((1, 1, RET_DK), F32)
    ones_v = jnp.ones((1, 1, RET_DV), F32)
    return dict(
        amat=amat,
        zf=jnp.exp((c - 1.0 - col) * lgf) * kscale * ones_k,
        zb=jnp.exp(col * lgb) * kscale * ones_k,
        xif=jnp.exp((col + 1.0) * lgf) * ones_v,
        xib=jnp.exp((c - col) * lgb) * ones_v,
        decf=jnp.exp(c * lgf) * ones_v,
        decb=jnp.exp(c * lgb) * ones_v,
    )


def _state_update(k_h, z_h, v_h):
    kz_t = jnp.transpose(k_h.astype(F32) * z_h).astype(BF16)
    return _dot(kz_t, v_h)


def _group_norm_gate(o, g_h):
    mu = jnp.mean(o, axis=-1, keepdims=True)
    d = o - mu
    var = jnp.mean(d * d, axis=-1, keepdims=True)
    g = g_h.astype(F32)
    return (_silu(g) * (d * lax.rsqrt(var + EPS))).astype(BF16)


def _ret_ctx_kernel(q_ref, k_ref, v_ref, g_ref, a_ref, zf_ref, zb_ref, y_ref, sf_ref, sb_ref):
    for h in range(RET_HEADS):
        q_h = q_ref[:, h * RET_DK:(h + 1) * RET_DK]
        k_h = k_ref[:, h * RET_DK:(h + 1) * RET_DK]
        v_h = v_ref[:, h * RET_DV:(h + 1) * RET_DV]
        inner = (_dot_nt(q_h, k_h) * a_ref[h]).astype(BF16)
        o = _dot(inner, v_h)
        y_ref[:, h * RET_DV:(h + 1) * RET_DV] = _group_norm_gate(o, g_ref[:, h * RET_DV:(h + 1) * RET_DV])
        sf_ref[h] = _state_update(k_h, zf_ref[h], v_h)
        sb_ref[h] = _state_update(k_h, zb_ref[h], v_h)


def _ret_ctx(proj, tabs, batch):
    c = RET_C
    qw, vw = RET_HEADS * RET_DK, RET_HEADS * RET_DV
    tab_spec = lambda arr: pl.BlockSpec(arr.shape, lambda b: (0, 0, 0))
    st_shape = jax.ShapeDtypeStruct((batch, RET_HEADS, RET_DK, RET_DV), F32)
    st_spec = pl.BlockSpec((None, RET_HEADS, RET_DK, RET_DV), lambda b: (b, 0, 0, 0))
    return pl.pallas_call(
        _ret_ctx_kernel,
        grid=(batch,),
        in_specs=[pl.BlockSpec((c, qw), lambda b: (b, 0)),
                  pl.BlockSpec((c, qw), lambda b: (b, 1)),
                  pl.BlockSpec((c, vw), lambda b: (b, 1)),
                  pl.BlockSpec((c, vw), lambda b: (b, 2)),
                  tab_spec(tabs["amat"]), tab_spec(tabs["zf"]), tab_spec(tabs["zb"])],
        out_specs=[pl.BlockSpec((c, vw), lambda b: (b, 0)), st_spec, st_spec],
        out_shape=[jax.ShapeDtypeStruct((batch * c, vw), BF16), st_shape, st_shape],
        compiler_params=_cparams(1),
        name="retention_context",
    )(proj, proj, proj, proj, tabs["amat"], tabs["zf"], tabs["zb"])


def _ret_bwd_kernel(q_ref, k_ref, v_ref, s0_ref, xib_ref, zb_ref, decb_ref, ob_ref, s_ref):
    @pl.when(pl.program_id(1) == 0)
    def _():
        s_ref[...] = s0_ref[...]

    for h in range(RET_HEADS):
        q_h = q_ref[:, h * RET_DK:(h + 1) * RET_DK]
        k_h = k_ref[:, h * RET_DK:(h + 1) * RET_DK]
        v_h = v_ref[:, h * RET_DV:(h + 1) * RET_DV]
        s = s_ref[h]
        ob_ref[:, h * RET_DV:(h + 1) * RET_DV] = _dot(q_h, s.astype(BF16)) * xib_ref[h]
        s_ref[h] = s * decb_ref[h] + _state_update(k_h, zb_ref[h], v_h)


def _ret_fwd_kernel(q_ref, k_ref, v_ref, g_ref, ob_ref, s0_ref, a_ref, xif_ref, zf_ref, decf_ref, y_ref, s_ref):
    @pl.when(pl.program_id(1) == 0)
    def _():
        s_ref[...] = s0_ref[...]

    for h in range(RET_HEADS):
        q_h = q_ref[:, h * RET_DK:(h + 1) * RET_DK]
        k_h = k_ref[:, h * RET_DK:(h + 1) * RET_DK]
        v_h = v_ref[:, h * RET_DV:(h + 1) * RET_DV]
        s = s_ref[h]
        inner = (_dot_nt(q_h, k_h) * a_ref[h]).astype(BF16)
        o = (_dot(inner, v_h) + _dot(q_h, s.astype(BF16)) * xif_ref[h]) + ob_ref[:, h * RET_DV:(h + 1) * RET_DV]
        y_ref[:, h * RET_DV:(h + 1) * RET_DV] = _group_norm_gate(o, g_ref[:, h * RET_DV:(h + 1) * RET_DV])
        s_ref[h] = s * decf_ref[h] + _state_update(k_h, zf_ref[h], v_h)


def _ret_lat(proj, tabs, s0_f, s0_b, batch, length):
    c = RET_C
    nc = length // c
    qw, vw = RET_HEADS * RET_DK, RET_HEADS * RET_DV
    t = batch * length
    tab_spec = lambda arr: pl.BlockSpec(arr.shape, lambda b, j: (0, 0, 0))
    st_spec = pl.BlockSpec((None, RET_HEADS, RET_DK, RET_DV), lambda b, j: (b, 0, 0, 0))
    st_scratch = pltpu.VMEM((RET_HEADS, RET_DK, RET_DV), F32)
    rev = lambda b, j: b * nc + (nc - 1 - j)
    fwd = lambda b, j: b * nc + j
    ob = pl.pallas_call(
        _ret_bwd_kernel,
        grid=(batch, nc),
        in_specs=[pl.BlockSpec((c, qw), lambda b, j: (rev(b, j), 0)),
                  pl.BlockSpec((c, qw), lambda b, j: (rev(b, j), 1)),
                  pl.BlockSpec((c, vw), lambda b, j: (rev(b, j), 1)),
                  st_spec, tab_spec(tabs["xib"]), tab_spec(tabs["zb"]), tab_spec(tabs["decb"])],
        out_specs=pl.BlockSpec((c, vw), lambda b, j: (rev(b, j), 0)),
        out_shape=jax.ShapeDtypeStruct((t, vw), F32),
        scratch_shapes=[st_scratch],
        compiler_params=_cparams(2),
        name="retention_latent_backward",
    )(proj, proj, proj, s0_b, tabs["xib"], tabs["zb"], tabs["decb"])
    return pl.pallas_call(
        _ret_fwd_kernel,
        grid=(batch, nc),
        in_specs=[pl.BlockSpec((c, qw), lambda b, j: (fwd(b, j), 0)),
                  pl.BlockSpec((c, qw), lambda b, j: (fwd(b, j), 1)),
                  pl.BlockSpec((c, vw), lambda b, j: (fwd(b, j), 1)),
                  pl.BlockSpec((c, vw), lambda b, j: (fwd(b, j), 2)),
                  pl.BlockSpec((c, vw), lambda b, j: (fwd(b, j), 0)),
                  st_spec, tab_spec(tabs["amat"]), tab_spec(tabs["xif"]), tab_spec(tabs["zf"]),
                  tab_spec(tabs["decf"])],
        out_specs=pl.BlockSpec((c, vw), lambda b, j: (fwd(b, j), 0)),
        out_shape=jax.ShapeDtypeStruct((t, vw), BF16),
        scratch_shapes=[st_scratch],
        compiler_params=_cparams(2),
        name="retention_latent_forward",
    )(proj, proj, proj, proj, ob, s0_f, tabs["amat"], tabs["xif"], tabs["zf"], tabs["decf"])


def _attn_ctx_kernel(sink_ref, q_ref, k_ref, v_ref, o_ref):
    scale = HEAD_DIM ** -0.5
    outs = []
    for g in range(ATTN_KV):
        k_g = k_ref[:, g * HEAD_DIM:(g + 1) * HEAD_DIM].astype(BF16)
        v_g = v_ref[:, g * HEAD_DIM:(g + 1) * HEAD_DIM].astype(BF16)
        for r in range(ATTN_REP):
            hh = g * ATTN_REP + r
            q_h = q_ref[:, hh * HEAD_DIM:(hh + 1) * HEAD_DIM].astype(BF16)
            s = _dot_nt(q_h, k_g) * scale
            snk = sink_ref[hh]
            m = jnp.maximum(jnp.max(s, axis=-1, keepdims=True), snk)
            p = jnp.exp(s - m)
            den = jnp.sum(p, axis=-1, keepdims=True) + jnp.exp(snk - m)
            outs.append(_dot(p.astype(BF16), v_g) / den)
    o_ref[...] = jnp.concatenate(outs, axis=1).astype(o_ref.dtype)


def _attn_ctx(qkv, sink, batch, length):
    qw = ATTN_HEADS * HEAD_DIM
    kw = ATTN_KV * HEAD_DIM
    return pl.pallas_call(
        _attn_ctx_kernel,
        grid=(batch,),
        in_specs=[pl.BlockSpec(memory_space=pltpu.SMEM),
                  pl.BlockSpec((length, qw), lambda b: (b, 0)),
                  pl.BlockSpec((length, kw), lambda b: (b, qw // kw)),
                  pl.BlockSpec((length, kw), lambda b: (b, qw // kw + 1))],
        out_specs=pl.BlockSpec((length, qw), lambda b: (b, 0)),
        out_shape=jax.ShapeDtypeStruct((batch * length, qw), BF16),
        compiler_params=_cparams(1),
        name="attention_context",
    )(sink, qkv, qkv, qkv)


def _attn_lat_kernel(sink_ref, q_ref, k_ref, v_ref, kc_ref, vc_ref, o_ref, *, length):
    scale = HEAD_DIM ** -0.5
    band = 3 * QBLK
    j = pl.program_id(1)
    start = pl.multiple_of(jnp.clip((j - 1) * QBLK, 0, length - band), QBLK)
    rows = ATTN_REP * QBLK
    qpos = j * QBLK + lax.broadcasted_iota(I32, (rows, band), 0) % QBLK
    kpos = start + lax.broadcasted_iota(I32, (rows, band), 1)
    valid = jnp.abs(qpos - kpos) <= WINDOW
    rep_of_row = lax.broadcasted_iota(I32, (rows, 1), 0) // QBLK
    kb = k_ref[pl.ds(start, band), :]
    vb = v_ref[pl.ds(start, band), :]
    outs = []
    for g in range(ATTN_KV):
        cs = slice(g * HEAD_DIM, (g + 1) * HEAD_DIM)
        k_g = kb[:, cs].astype(BF16)
        v_g = vb[:, cs].astype(BF16)
        kc_g = kc_ref[:, cs].astype(BF16)
        vc_g = vc_ref[:, cs].astype(BF16)
        q_g = jnp.concatenate(
            [q_ref[:, (g * ATTN_REP + r) * HEAD_DIM:(g * ATTN_REP + r + 1) * HEAD_DIM] for r in range(ATTN_REP)],
            axis=0).astype(BF16)
        snk = jnp.zeros((rows, 1), F32)
        for r in range(ATTN_REP):
            snk = jnp.where(rep_of_row == r, sink_ref[g * ATTN_REP + r], snk)
        s_loc = jnp.where(valid, _dot_nt(q_g, k_g) * scale, NEG_INF)
        s_ctx = _dot_nt(q_g, kc_g) * scale
        m = jnp.maximum(jnp.maximum(jnp.max(s_loc, axis=-1, keepdims=True),
                                    jnp.max(s_ctx, axis=-1, keepdims=True)), snk)
        p_loc = jnp.exp(s_loc - m)
        p_ctx = jnp.exp(s_ctx - m)
        den = (jnp.sum(p_loc, axis=-1, keepdims=True) + jnp.sum(p_ctx, axis=-1, keepdims=True)) + jnp.exp(snk - m)
        o_g = (_dot(p_loc.astype(BF16), v_g) + _dot(p_ctx.astype(BF16), vc_g)) / den
        for r in range(ATTN_REP):
            outs.append(o_g[r * QBLK:(r + 1) * QBLK, :])
    o_ref[...] = jnp.concatenate(outs, axis=1).astype(o_ref.dtype)


def _attn_lat(qkv, sink, kc, vc, batch, length):
    qw = ATTN_HEADS * HEAD_DIM
    kw = ATTN_KV * HEAD_DIM
    nq = length // QBLK
    past = kc.shape[1]
    return pl.pallas_call(
        functools.partial(_attn_lat_kernel, length=length),
        grid=(batch, nq),
        in_specs=[pl.BlockSpec(memory_space=pltpu.SMEM),
                  pl.BlockSpec((QBLK, qw), lambda b, j: (b * nq + j, 0)),
                  pl.BlockSpec((length, kw), lambda b, j: (b, qw // kw)),
                  pl.BlockSpec((length, kw), lambda b, j: (b, qw // kw + 1)),
                  pl.BlockSpec((None, past, kw), lambda b, j: (b, 0, 0)),
                  pl.BlockSpec((None, past, kw), lambda b, j: (b, 0, 0))],
        out_specs=pl.BlockSpec((QBLK, qw), lambda b, j: (b * nq + j, 0)),
        out_shape=jax.ShapeDtypeStruct((batch * length, qw), BF16),
        compiler_params=_cparams(2),
        name="attention_latent",
    )(sink, qkv, qkv, qkv, kc, vc)


def _rope_tables(length):
    axis_dim = HEAD_DIM // 2
    pos = jnp.arange(length)
    row = (pos // GRID_W).astype(F32)
    colp = (pos % GRID_W).astype(F32)
    inv = ROPE_BASE ** (-jnp.arange(0, axis_dim, 2, dtype=F32) / axis_dim)
    ar = row[:, None] * inv[None, :]
    ac = colp[:, None] * inv[None, :]
    cos = jnp.concatenate([jnp.cos(ar), jnp.cos(ar), jnp.cos(ac), jnp.cos(ac)], axis=1)
    sin = jnp.concatenate([-jnp.sin(ar), jnp.sin(ar), -jnp.sin(ac), jnp.sin(ac)], axis=1)
    return jnp.concatenate([cos, cos], axis=1), jnp.concatenate([sin, sin], axis=1)


def _router_kernel(x_ref, g_ref, sh_ref, sc_ref, wr_ref, br_ref, h_ref, e_ref, p1_ref, p2_ref, n_ref, cnt_ref,
                   *, tiles_per_chunk):
    h = _norm_mod(x_ref[...], g_ref[...], sh_ref[...], sc_ref[...])
    _to_slabs(h_ref, h)
    logits = _dot3(h, wr_ref[...]) + br_ref[...]
    lane = lax.broadcasted_iota(I32, logits.shape, 1).astype(F32)
    ninf = -jnp.inf
    lg = jnp.where(lane < N_GROUPS, logits, ninf)
    gmax = jnp.max(lg, axis=-1, keepdims=True)
    gtop = jnp.min(jnp.where(lg == gmax, lane, float(LANE)), axis=-1, keepdims=True)
    pg_sel = 1.0 / jnp.sum(jnp.exp(lg - gmax), axis=-1, keepdims=True)
    lo = N_GROUPS + EXP_PER_GROUP * gtop
    le = jnp.where((lane >= lo) & (lane < lo + EXP_PER_GROUP), logits, ninf)
    v1 = jnp.max(le, axis=-1, keepdims=True)
    i1 = jnp.min(jnp.where(le == v1, lane, float(LANE)), axis=-1, keepdims=True)
    le2 = jnp.where(lane == i1, ninf, le)
    v2 = jnp.max(le2, axis=-1, keepdims=True)
    i2 = jnp.min(jnp.where(le2 == v2, lane, float(LANE)), axis=-1, keepdims=True)
    t = jnp.exp(v2 - v1)
    gate1 = pg_sel * (1.0 / (1.0 + t))
    gate2 = pg_sel * (t / (1.0 + t))
    p1_ref[...] = jnp.broadcast_to(gate1, p1_ref.shape)
    p2_ref[...] = jnp.broadcast_to(gate2, p2_ref.shape)

    e1 = i1 - N_GROUPS
    e2 = i2 - N_GROUPS
    tm = logits.shape[0]

    @pl.when(pl.program_id(0) % tiles_per_chunk == 0)
    def _():
        cnt_ref[...] = jnp.zeros_like(cnt_ref)

    carry = cnt_ref[0:1, :]
    oh1 = (lane == e1).astype(F32)
    oh2 = (lane == e2).astype(F32)
    tri = (lax.broadcasted_iota(I32, (tm, tm), 0) >= lax.broadcasted_iota(I32, (tm, tm), 1)).astype(BF16)
    cs1 = _dot(tri, oh1.astype(BF16))
    cs2 = _dot(tri, oh2.astype(BF16))
    tot1 = cs1[tm - 1:tm, :]
    tot2 = cs2[tm - 1:tm, :]
    rank1 = jnp.sum(oh1 * (carry + cs1 - 1.0), axis=-1, keepdims=True)
    rank2 = jnp.sum(oh2 * ((carry + tot1) + cs2 - 1.0), axis=-1, keepdims=True)
    code1 = rank1 * float(N_EXP) + e1
    code2 = rank2 * float(N_EXP) + e2
    e_ref[...] = jnp.where(lane == 0, code1, jnp.where(lane == 1, code2, 0.0)).astype(I32)
    new_cnt = (carry + tot1) + tot2
    cnt_ref[...] = jnp.broadcast_to(new_cnt, cnt_ref.shape)
    n_ref[...] = jnp.broadcast_to(new_cnt, n_ref.shape).astype(I32)


def _router(x, mods, layer, norm_g, wr, br, is_lat):
    t = x.shape[0]
    tm = 256
    seg_fn = _seg_of(is_lat, tm)
    tiles_per_chunk = SEG // tm
    return pl.pallas_call(
        functools.partial(_router_kernel, tiles_per_chunk=tiles_per_chunk),
        grid=(t // tm,),
        in_specs=[pl.BlockSpec((tm, D), lambda i: (i, 0)),
                  pl.BlockSpec((1, D), lambda i: (0, 0)),
                  _mod_spec(layer, 3, seg_fn),
                  _mod_spec(layer, 4, seg_fn),
                  pl.BlockSpec((D, LANE), lambda i: (0, 0)),
                  pl.BlockSpec((1, LANE), lambda i: (0, 0))],
        out_specs=[pl.BlockSpec((tm * SUB, LANE), lambda i: (i, 0)),
                   pl.BlockSpec((tm, LANE), lambda i: (i, 0)),
                   pl.BlockSpec((tm, LANE), lambda i: (i, 0)),
                   pl.BlockSpec((tm, LANE), lambda i: (i, 0)),
                   pl.BlockSpec((SUB, LANE), lambda i: (i // tiles_per_chunk, 0))],
        out_shape=[jax.ShapeDtypeStruct((t * SUB, LANE), F32),
                   jax.ShapeDtypeStruct((t, LANE), I32),
                   jax.ShapeDtypeStruct((t, LANE), F32),
                   jax.ShapeDtypeStruct((t, LANE), F32),
                   jax.ShapeDtypeStruct((t // SEG * SUB, LANE), I32)],
        scratch_shapes=[pltpu.VMEM((SUB, LANE), F32)],
        compiler_params=_cparams(1),
        name="moe_router",
    )(x, norm_g.reshape(1, D), mods, mods, wr, br)


def _moe_kernel(dest_ref, meta_ref, h_hbm, g1_hbm, g2_hbm, w1_ref, w3_ref, w2_ref, y_hbm,
                h_vm, y_vm, g_vm, xt, ot, gt, w1b, w3b, w2b, slot_tab, sem, *, tc, a_pad):
    chunk = pl.program_id(0)
    e = pl.program_id(1)
    stride = MOE_TILE_STRIDE
    n_lane_chunks = D // LANE
    h_rows = tc * SUB
    n_assign = 2 * tc
    slab_mask = (1 << 16) - 1

    def h_copy():
        return pltpu.make_async_copy(h_hbm.at[pl.ds(chunk * h_rows, h_rows)], h_vm.at[pl.ds(0, h_rows)], sem.at[0])

    def y_copy():
        return pltpu.make_async_copy(y_vm.at[pl.ds(0, h_rows)], y_hbm.at[pl.ds(chunk * h_rows, h_rows)], sem.at[1])

    def g_copy(k, g_hbm):
        return pltpu.make_async_copy(g_hbm.at[pl.ds(chunk * tc, tc)], g_vm.at[pl.ds(k * tc, tc)], sem.at[2 + k])

    @pl.when(e == 0)
    def _():
        h_copy().start()
        g_copy(0, g1_hbm).start()
        g_copy(1, g2_hbm).start()

        def zero(i, carry):
            y_vm[pl.ds(pl.multiple_of(i * 512, 512), 512), :] = jnp.zeros((512, LANE), F32)
            return carry

        lax.fori_loop(0, h_rows // 512, zero, 0)
        y_vm[pl.ds(h_rows, SUB), :] = jnp.zeros((SUB, LANE), F32)
        h_vm[pl.ds(h_rows, SUB), :] = jnp.zeros((SUB, LANE), F32)

        def fill(i, carry):
            for u in range(SUB):
                slot_tab[i * SUB + u] = h_rows
            return carry

        lax.fori_loop(0, a_pad // SUB, fill, 0)

        def invert(i, carry):
            for u in range(SUB):
                tok = i * (SUB // 2) + u // 2
                code = dest_ref[chunk * n_assign + i * SUB + u]
                slot = meta_ref[chunk * 2 * N_EXP + (code & (N_EXP - 1))] + lax.shift_right_logical(code, 5)
                slot_tab[slot] = (tok * SUB) | (((u % 2) * tc + tok) << 16)
            return carry

        lax.fori_loop(0, n_assign // SUB, invert, 0)
        h_copy().wait()
        g_copy(0, g1_hbm).wait()
        g_copy(1, g2_hbm).wait()

    w1b[...] = w1_ref[...].astype(BF16)
    w3b[...] = w3_ref[...].astype(BF16)
    w2b[...] = w2_ref[...].astype(BF16)

    start = meta_ref[chunk * 2 * N_EXP + e]
    n_pad = meta_ref[chunk * 2 * N_EXP + N_EXP + e]

    def process(off, m):
        off = pl.multiple_of(off, MOE_PAD)

        def gather(i, carry):
            for u in range(SUB):
                r = i * SUB + u
                entry = slot_tab[off + r]
                t8 = pl.multiple_of(entry & slab_mask, SUB)
                xt[pl.ds(r, SUB, stride=stride), :] = h_vm[pl.ds(t8, SUB), :]
                gt[pl.ds(r, 1), :] = g_vm[pl.ds(lax.shift_right_logical(entry, 16), 1), :]
            return carry

        lax.fori_loop(0, m // SUB, gather, 0)
        x = jnp.concatenate([xt[pl.ds(c * stride, m), :] for c in range(n_lane_chunks)], axis=1).astype(BF16)
        a = _dot(x, w1b[...])
        b = _dot(x, w3b[...])
        hm = (_silu(a) * b).astype(BF16)
        out = _dot(hm, w2b[...])
        gate = gt[pl.ds(0, m), :]
        for c in range(n_lane_chunks):
            ot[pl.ds(c * stride, m), :] = out[:, c * LANE:(c + 1) * LANE] * gate

        def scatter(i, carry):
            t8s, news = [], []
            for u in range(4):
                r = i * 4 + u
                t8 = pl.multiple_of(slot_tab[off + r] & slab_mask, SUB)
                t8s.append(t8)
                news.append(y_vm[pl.ds(t8, SUB), :] + ot[pl.ds(r, SUB, stride=stride), :])
            for u in range(4):
                y_vm[pl.ds(t8s[u], SUB), :] = news[u]
            return carry

        lax.fori_loop(0, m // 4, scatter, 0)

    n_full = n_pad // 256

    def full_body(i, carry):
        process(start + i * 256, 256)
        return carry

    lax.fori_loop(0, n_full, full_body, 0)
    rem = start + n_full * 256

    @pl.when((n_pad & 128) != 0)
    def _():
        process(rem, 128)

    @pl.when((n_pad & 64) != 0)
    def _():
        process(rem + (n_pad & 128), 64)

    @pl.when(e == N_EXP - 1)
    def _():
        y_copy().start()
        y_copy().wait()


def _moe(h_slabs, codes, counts, gate1, gate2, w1, w3, w2, layer):
    t = codes.shape[0]
    tc = SEG
    n_chunks = t // tc
    a_pad = 2 * tc + N_EXP * MOE_PAD
    padded = (counts + MOE_PAD - 1) // MOE_PAD * MOE_PAD
    meta = jnp.concatenate([jnp.cumsum(padded, axis=1) - padded, padded], axis=1).astype(I32)
    dest = codes
    slab_rows = (tc + 1) * SUB
    tile_rows = (D // LANE) * MOE_TILE_STRIDE
    hbm_spec = pl.BlockSpec(memory_space=pl.ANY)
    grid_spec = pltpu.PrefetchScalarGridSpec(
        num_scalar_prefetch=2,
        grid=(n_chunks, N_EXP),
        in_specs=[hbm_spec, hbm_spec, hbm_spec,
                  pl.BlockSpec((None, None, D, D_FF), lambda c, e, *_: (layer, e, 0, 0)),
                  pl.BlockSpec((None, None, D, D_FF), lambda c, e, *_: (layer, e, 0, 0)),
                  pl.BlockSpec((None, None, D_FF, D), lambda c, e, *_: (layer, e, 0, 0))],
        out_specs=pl.BlockSpec(memory_space=pl.ANY),
        scratch_shapes=[pltpu.VMEM((slab_rows, LANE), F32),
                        pltpu.VMEM((slab_rows, LANE), F32),
                        pltpu.VMEM((2 * tc, LANE), F32),
                        pltpu.VMEM((tile_rows, LANE), F32),
                        pltpu.VMEM((tile_rows, LANE), F32),
                        pltpu.VMEM((256, LANE), F32),
                        pltpu.VMEM((D, D_FF), BF16),
                        pltpu.VMEM((D, D_FF), BF16),
                        pltpu.VMEM((D_FF, D), BF16),
                        pltpu.SMEM((a_pad,), I32),
                        pltpu.SemaphoreType.DMA((4,))],
    )
    return pl.pallas_call(
        functools.partial(_moe_kernel, tc=tc, a_pad=a_pad),
        grid_spec=grid_spec,
        out_shape=jax.ShapeDtypeStruct((t * SUB, LANE), F32),
        compiler_params=pltpu.CompilerParams(dimension_semantics=("arbitrary", "arbitrary"),
                                             vmem_limit_bytes=V7X_VMEM_LIMIT_MOE),
        name="moe_experts",
    )(dest.reshape(n_chunks * 2 * tc), meta.reshape(n_chunks * 2 * N_EXP), h_slabs, gate1, gate2, w1, w3, w2)


def kernel(x_prompt, x_sample, c, cache_attn_k, cache_attn_v, state_ret_fwd, state_ret_bwd, c_ctx, adaln_w, adaln_b, norm1_g, norm2_g, final_g, fnet_w, ret_w_in, ret_w_out, ret_decay_fwd, ret_decay_bwd, attn_w_in, attn_w_out, attn_sink, router_group_w, router_group_b, router_expert_w, router_expert_b, moe_w1, moe_w3, moe_w2):
    bp, lp, _ = x_prompt.shape
    bs, ls, _ = x_sample.shape
    xs = {False: x_prompt.reshape(bp * lp, D), True: x_sample.reshape(bs * ls, D)}
    dims = {False: (bp, lp), True: (bs, ls)}

    cond8 = jnp.concatenate([c_ctx[None, :], c, jnp.zeros((SUB - 1 - bs, D), F32)], axis=0)
    mods = _adaln(cond8, adaln_w, adaln_b).reshape(DEPTH, SUB, 6, 1, D)

    rope_lat = _rope_tables(ls)
    rope = {False: (jnp.ones((bp * lp, LANE), F32), jnp.zeros((bp * lp, LANE), F32)),
            True: (jnp.tile(rope_lat[0], (bs, 1)), jnp.tile(rope_lat[1], (bs, 1)))}

    extras = {}
    for i in range(DEPTH):
        kind, j = i % 3, i // 3
        if kind == 0:
            lw = dict(w_fold=_fnet_fold(fnet_w[j]))
        elif kind == 1:
            lw = dict(w_in=ret_w_in[j].astype(BF16), w_out=ret_w_out[j].astype(BF16),
                      tabs=_ret_tables(ret_decay_fwd[j], ret_decay_bwd[j]),
                      s0_f=state_ret_fwd[:, j], s0_b=state_ret_bwd[:, j])
        else:
            kvw = ATTN_KV * HEAD_DIM
            lw = dict(w_in=attn_w_in[j].astype(BF16), w_out=attn_w_out[j].astype(BF16), sink=attn_sink[j],
                      kc=cache_attn_k[:, j].reshape(bs, -1, kvw), vc=cache_attn_v[:, j].reshape(bs, -1, kvw))
        wr = jnp.concatenate([router_group_w[i], jnp.transpose(router_expert_w[i], (1, 0, 2)).reshape(D, N_EXP),
                              jnp.zeros((D, LANE - N_GROUPS - N_EXP), F32)], axis=1)
        br = jnp.concatenate([router_group_b[i], router_expert_b[i].reshape(N_EXP),
                              jnp.zeros((LANE - N_GROUPS - N_EXP,), F32)]).reshape(1, LANE)

        for is_lat in (False, True):
            b_, l_ = dims[is_lat]
            x = _mixer(i, xs[is_lat], mods, norm1_g[i], lw, is_lat, b_, l_, rope[is_lat], extras)
            xs[is_lat] = _ffn(i, x, mods, norm2_g[i], wr, br, moe_w1, moe_w3, moe_w2, is_lat)

    y_prompt = _final_norm(xs[False], final_g).reshape(bp, lp, D)
    y_sample = _final_norm(xs[True], final_g).reshape(bs, ls, D)
    return (y_prompt, y_sample, extras["k"], extras["v"], extras["sf"], extras["sb"])


def _mixer(i, x, mods, norm_g, lw, is_lat, b_, l_, rope, extras):
    kind = i % 3
    if kind == 0:
        y = _proj(x, mods, i, 0, norm_g, lw["w_fold"], is_lat, BF16)
        return _seqmix(y, x, mods, i, is_lat, b_, l_)
    if kind == 1:
        proj = _proj(x, mods, i, 0, norm_g, lw["w_in"], is_lat, BF16)
        if is_lat:
            yr = _ret_lat(proj, lw["tabs"], lw["s0_f"], lw["s0_b"], b_, l_)
        else:
            yr, sf, sb = _ret_ctx(proj, lw["tabs"], b_)
            extras["sf"], extras["sb"] = sf[:, None], sb[:, None]
        return _linres(yr, lw["w_out"], x, mods, i, 2, is_lat)
    qkv = _proj(x, mods, i, 0, norm_g, lw["w_in"], is_lat, F32, rope=rope)
    if is_lat:
        o = _attn_lat(qkv, lw["sink"], lw["kc"], lw["vc"], b_, l_)
    else:
        qw = ATTN_HEADS * HEAD_DIM
        kw = ATTN_KV * HEAD_DIM
        extras["k"] = qkv[:, qw:qw + kw].reshape(b_, 1, l_, ATTN_KV, HEAD_DIM)
        extras["v"] = qkv[:, qw + kw:].reshape(b_, 1, l_, ATTN_KV, HEAD_DIM)
        o = _attn_ctx(qkv, lw["sink"], b_, l_)
    return _linres(o, lw["w_out"], x, mods, i, 2, is_lat)


def _ffn(i, x, mods, norm_g, wr, br, w1, w3, w2, is_lat):
    h_slabs, codes, gate1, gate2, counts = _router(x, mods, i, norm_g, wr, br, is_lat)
    counts = counts.reshape(-1, SUB, LANE)[:, 0, :N_EXP]
    y_slabs = _moe(h_slabs, codes[:, :2], counts, gate1, gate2, w1, w3, w2, i)
    return _residual(x, y_slabs, mods, i, 5, is_lat)
```

```python
import functools
import math

import jax
import jax.numpy as jnp
from jax import lax
from jax.experimental import pallas as pl
from jax.experimental.pallas import tpu as pltpu

F32 = jnp.float32
BF16 = jnp.bfloat16
I32 = jnp.int32

D = 1024
DEPTH = 4
SEG = 4096
FNET_GROUP_DIM = 256
RET_HEADS, RET_DK, RET_DV = 4, 256, 512
RET_C = 256
ATTN_HEADS, ATTN_KV, HEAD_DIM, ATTN_REP = 16, 4, 64, 4
WINDOW, QBLK = 128, 128
GRID_W = 64
ROPE_BASE = 10000.0
N_EXP, EXP_PER_GROUP, N_GROUPS, D_FF = 32, 8, 4, 512
EPS = 1e-6
NEG_INF = -1e30
LANE = 128
SUB = 8

V7X_VMEM_LIMIT = 56 * 1024 * 1024
V7X_VMEM_LIMIT_MOE = 59 * 1024 * 1024
MOE_PAD = 64
MOE_TILE_STRIDE = 264


def _cparams(n_axes):
    return pltpu.CompilerParams(dimension_semantics=("arbitrary",) * n_axes, vmem_limit_bytes=V7X_VMEM_LIMIT)


def _dot(a, b):
    return jnp.dot(a, b, preferred_element_type=F32)


def _dot_nt(a, b):
    return lax.dot_general(a, b, (((1,), (1,)), ((), ())), preferred_element_type=F32)


def _split(x):
    hi = x.astype(BF16)
    lo = (x - hi.astype(F32)).astype(BF16)
    return hi, lo


def _dot3(a, b):
    ah, al = _split(a)
    bh, bl = _split(b)
    return _dot(ah, bh) + (_dot(ah, bl) + _dot(al, bh))


def _silu(x):
    return x * jax.nn.sigmoid(x)


def _norm_mod(x, g, sh, sc):
    ms = jnp.mean(x * x, axis=-1, keepdims=True)
    return (x * lax.rsqrt(ms + EPS) * g) * (1.0 + sc) + sh


def _seg_of(is_lat, tm):
    if is_lat:
        return lambda i: 1 + (i * tm) // SEG
    return lambda i: 0


def _mod_spec(layer, kind, seg_fn):
    return pl.BlockSpec((None, None, None, 1, D), lambda i, *_: (layer, seg_fn(i), kind, 0, 0))


def _adaln_kernel(c_ref, w_ref, b_ref, o_ref):
    o_ref[...] = _dot3(_silu(c_ref[...]), w_ref[...]) + b_ref[...]


def _adaln(cond8, w, b):
    depth, _, n = w.shape
    tn = 1536
    return pl.pallas_call(
        _adaln_kernel,
        grid=(depth, n // tn),
        in_specs=[pl.BlockSpec((SUB, D), lambda l, j: (0, 0)),
                  pl.BlockSpec((None, D, tn), lambda l, j: (l, 0, j)),
                  pl.BlockSpec((None, 1, tn), lambda l, j: (l, 0, j))],
        out_specs=pl.BlockSpec((None, SUB, tn), lambda l, j: (l, 0, j)),
        out_shape=jax.ShapeDtypeStruct((depth, SUB, n), F32),
        compiler_params=_cparams(2),
        name="adaln",
    )(cond8, w, b.reshape(depth, 1, n))


def _rope(x, cos_t, sin_t):
    w = x.shape[1]
    reps = w // LANE
    c = jnp.concatenate([cos_t] * reps, axis=1)
    s = jnp.concatenate([sin_t] * reps, axis=1)
    lane = lax.broadcasted_iota(I32, x.shape, 1)
    first = (lane % 32) < 16
    partner = jnp.where(first, pltpu.roll(x, w - 16, 1), pltpu.roll(x, 16, 1))
    return x * c + partner * s


def _proj_kernel(x_ref, g_ref, sh_ref, sc_ref, w_ref, *rest, n_rope):
    if n_rope:
        cos_ref, sin_ref, o_ref = rest
    else:
        (o_ref,) = rest
    hb = _norm_mod(x_ref[...], g_ref[...], sh_ref[...], sc_ref[...]).astype(BF16)
    n = w_ref.shape[1]
    tn = 512
    if n_rope:
        q_w = ATTN_HEADS * HEAD_DIM
        k_w = ATTN_KV * HEAD_DIM
        q = _dot(hb, w_ref[:, :q_w])
        o_ref[:, :q_w] = _rope(q, cos_ref[...], sin_ref[...])
        k = _dot(hb, w_ref[:, q_w:q_w + k_w])
        o_ref[:, q_w:q_w + k_w] = _rope(k, cos_ref[...], sin_ref[...])
        o_ref[:, q_w + k_w:] = _dot(hb, w_ref[:, q_w + k_w:])
    else:
        for j in range(n // tn):
            o_ref[:, j * tn:(j + 1) * tn] = _dot(hb, w_ref[:, j * tn:(j + 1) * tn]).astype(o_ref.dtype)


def _proj(x, mods, layer, sub, norm_g, w_bf16, is_lat, out_dtype, rope=None):
    t = x.shape[0]
    n = w_bf16.shape[1]
    tm = 256
    seg_fn = _seg_of(is_lat, tm)
    in_specs = [pl.BlockSpec((tm, D), lambda i: (i, 0)),
                pl.BlockSpec((1, D), lambda i: (0, 0)),
                _mod_spec(layer, 3 * sub + 0, seg_fn),
                _mod_spec(layer, 3 * sub + 1, seg_fn),
                pl.BlockSpec((D, n), lambda i: (0, 0))]
    args = [x, norm_g.reshape(1, D), mods, mods, w_bf16]
    if rope is not None:
        in_specs += [pl.BlockSpec((tm, LANE), lambda i: (i, 0))] * 2
        args += list(rope)
    return pl.pallas_call(
        functools.partial(_proj_kernel, n_rope=rope is not None),
        grid=(t // tm,),
        in_specs=in_specs,
        out_specs=pl.BlockSpec((tm, n), lambda i: (i, 0)),
        out_shape=jax.ShapeDtypeStruct((t, n), out_dtype),
        compiler_params=_cparams(1),
        name="norm_mod_proj",
    )(*args)


def _linres_kernel(y_ref, w_ref, x_ref, g_ref, o_ref):
    o_ref[...] = x_ref[...] + g_ref[...] * _dot(y_ref[...].astype(BF16), w_ref[...])


def _linres(y, w_bf16, x, mods, layer, kind, is_lat):
    t, k = y.shape
    tm = 256
    seg_fn = _seg_of(is_lat, tm)
    return pl.pallas_call(
        _linres_kernel,
        grid=(t // tm,),
        in_specs=[pl.BlockSpec((tm, k), lambda i: (i, 0)),
                  pl.BlockSpec((k, D), lambda i: (0, 0)),
                  pl.BlockSpec((tm, D), lambda i: (i, 0)),
                  _mod_spec(layer, kind, seg_fn)],
        out_specs=pl.BlockSpec((tm, D), lambda i: (i, 0)),
        out_shape=jax.ShapeDtypeStruct((t, D), F32),
        compiler_params=_cparams(1),
        name="linear_residual",
    )(y, w_bf16, x, mods)


def _to_slabs(slab_ref, x):
    tm = x.shape[0]
    for c in range(D // LANE):
        slab_ref[pl.ds(c, tm, stride=SUB), :] = x[:, c * LANE:(c + 1) * LANE]


def _from_slabs(slab_ref, tm):
    return jnp.concatenate([slab_ref[pl.ds(c, tm, stride=SUB), :] for c in range(D // LANE)], axis=1)


def _residual_kernel(x_ref, y_ref, g_ref, o_ref):
    o_ref[...] = x_ref[...] + g_ref[...] * _from_slabs(y_ref, x_ref.shape[0])


def _residual(x, y_slabs, mods, layer, kind, is_lat):
    t = x.shape[0]
    tm = 512
    seg_fn = _seg_of(is_lat, tm)
    return pl.pallas_call(
        _residual_kernel,
        grid=(t // tm,),
        in_specs=[pl.BlockSpec((tm, D), lambda i: (i, 0)),
                  pl.BlockSpec((tm * SUB, LANE), lambda i: (i, 0)),
                  _mod_spec(layer, kind, seg_fn)],
        out_specs=pl.BlockSpec((tm, D), lambda i: (i, 0)),
        out_shape=jax.ShapeDtypeStruct((t, D), F32),
        compiler_params=_cparams(1),
        name="gated_residual",
    )(x, y_slabs, mods)


def _final_norm_kernel(x_ref, g_ref, o_ref):
    x = x_ref[...]
    ms = jnp.mean(x * x, axis=-1, keepdims=True)
    o_ref[...] = x * lax.rsqrt(ms + EPS) * g_ref[...]


def _final_norm(x, g):
    t = x.shape[0]
    tm = 512
    return pl.pallas_call(
        _final_norm_kernel,
        grid=(t // tm,),
        in_specs=[pl.BlockSpec((tm, D), lambda i: (i, 0)), pl.BlockSpec((1, D), lambda i: (0, 0))],
        out_specs=pl.BlockSpec((tm, D), lambda i: (i, 0)),
        out_shape=jax.ShapeDtypeStruct((t, D), F32),
        compiler_params=_cparams(1),
        name="final_rmsnorm",
    )(x, g.reshape(1, D))


def _fold_kernel(c_ref, s_ref, w_ref, o_ref):
    w = w_ref[...]
    o_ref[:, :D] = _dot3(c_ref[...], w).astype(BF16)
    o_ref[:, D:] = _dot3(s_ref[...], w).astype(BF16)


def _fnet_fold(w):
    n = jnp.arange(FNET_GROUP_DIM, dtype=I32)
    ang = ((n[:, None] * n[None, :]) % FNET_GROUP_DIM).astype(F32) * (2.0 * math.pi / FNET_GROUP_DIM)
    sc = 1.0 / math.sqrt(FNET_GROUP_DIM)
    gd = FNET_GROUP_DIM
    return pl.pallas_call(
        _fold_kernel,
        grid=(D // gd,),
        in_specs=[pl.BlockSpec((gd, gd), lambda g: (0, 0)),
                  pl.BlockSpec((gd, gd), lambda g: (0, 0)),
                  pl.BlockSpec((gd, D), lambda g: (g, 0))],
        out_specs=pl.BlockSpec((gd, 2 * D), lambda g: (g, 0)),
        out_shape=jax.ShapeDtypeStruct((D, 2 * D), BF16),
        compiler_params=_cparams(1),
        name="fnet_fold_channel_dft",
    )(jnp.cos(ang) * sc, jnp.sin(ang) * sc, w)


def _seq_dft(length):
    n = jnp.arange(length, dtype=I32)
    sc = 1.0 / math.sqrt(length)
    split = 64
    if length <= split * 4:
        ang = ((n[:, None] * n[None, :]) % length).astype(F32) * (2.0 * math.pi / length)
        return (jnp.cos(ang) * sc).astype(BF16), (jnp.sin(ang) * (-sc)).astype(BF16)
    hi = jnp.arange(length // split, dtype=I32)
    lo = jnp.arange(split, dtype=I32)
    ang_a = ((n[:, None] * hi[None, :]) % (length // split)).astype(F32) * (2.0 * math.pi * split / length)
    ang_b = ((n[:, None] * lo[None, :]) % length).astype(F32) * (2.0 * math.pi / length)
    ca, sa = jnp.cos(ang_a)[:, :, None], jnp.sin(ang_a)[:, :, None]
    cb, sb = (jnp.cos(ang_b) * sc)[:, None, :], (jnp.sin(ang_b) * sc)[:, None, :]
    cm = (ca * cb - sa * sb).reshape(length, length)
    sm = (sa * cb + ca * sb).reshape(length, length)
    return cm.astype(BF16), (-sm).astype(BF16)


def _seqmix_kernel(c_ref, s_ref, y_ref, x_ref, g_ref, o_ref, acc_ref, *, nk):
    k = pl.program_id(2)

    @pl.when(k == 0)
    def _():
        acc_ref[...] = jnp.zeros_like(acc_ref)

    acc_ref[...] += _dot(c_ref[...], y_ref[:, :D]) + _dot(s_ref[...], y_ref[:, D:])

    @pl.when(k == nk - 1)
    def _():
        o_ref[...] = x_ref[...] + g_ref[...] * acc_ref[...]


def _seqmix(y, x, mods, layer, is_lat, batch, length):
    cm, sm = _seq_dft(length)
    tm = min(length, 1024)
    tk = min(length, 1024)
    nm, nk = length // tm, length // tk
    seg = (lambda b: 1 + b) if is_lat else (lambda b: 0)
    out = pl.pallas_call(
        functools.partial(_seqmix_kernel, nk=nk),
        grid=(batch, nm, nk),
        in_specs=[pl.BlockSpec((tm, tk), lambda b, m, k: (m, k)),
                  pl.BlockSpec((tm, tk), lambda b, m, k: (m, k)),
                  pl.BlockSpec((None, tk, 2 * D), lambda b, m, k: (b, k, 0)),
                  pl.BlockSpec((None, tm, D), lambda b, m, k: (b, m, 0)),
                  pl.BlockSpec((None, None, None, 1, D), lambda b, m, k: (layer, seg(b), 2, 0, 0))],
        out_specs=pl.BlockSpec((None, tm, D), lambda b, m, k: (b, m, 0)),
        out_shape=jax.ShapeDtypeStruct((batch, length, D), F32),
        scratch_shapes=[pltpu.VMEM((tm, D), F32)],
        compiler_params=_cparams(3),
        name="fnet_seq_dft",
    )(cm, sm, y.reshape(batch, length, 2 * D), x.reshape(batch, length, D), mods)
    return out.reshape(batch * length, D)


def _ret_tables(dec_f, dec_b):
    c = RET_C
    lgf = jax.nn.log_sigmoid(dec_f.astype(F32))[:, None, None]
    lgb = jax.nn.log_sigmoid(dec_b.astype(F32))[:, None, None]
    idx = jnp.arange(c, dtype=F32)
    diff = idx[:, None] - idx[None, :]
    kscale = RET_DK ** -0.5
    amat = jnp.where(diff >= 0, jnp.exp(jnp.maximum(diff, 0.0)[None] * lgf),
                     jnp.exp(jnp.maximum(-diff, 0.0)[None] * lgb)) * kscale
    col = idx[None, :, None]
    ones_k = jnp.ones((1, 1, RET_DK), F32)
    ones_v = jnp.ones((1, 1, RET_DV), F32)
    return dict(
        amat=amat,
        zf=jnp.exp((c - 1.0 - col) * lgf) * kscale * ones_k,
        zb=jnp.exp(col * lgb) * kscale * ones_k,
        xif=jnp.exp((col + 1.0) * lgf) * ones_v,
        xib=jnp.exp((c - col) * lgb) * ones_v,
        decf=jnp.exp(c * lgf) * ones_v,
        decb=jnp.exp(c * lgb) * ones_v,
    )


def _state_update(k_h, z_h, v_h):
    kz_t = jnp.transpose(k_h.astype(F32) * z_h).astype(BF16)
    return _dot(kz_t, v_h)


def _group_norm_gate(o, g_h):
    mu = jnp.mean(o, axis=-1, keepdims=True)
    d = o - mu
    var = jnp.mean(d * d, axis=-1, keepdims=True)
    g = g_h.astype(F32)
    return (_silu(g) * (d * lax.rsqrt(var + EPS))).astype(BF16)


def _ret_ctx_kernel(q_ref, k_ref, v_ref, g_ref, a_ref, zf_ref, zb_ref, y_ref, sf_ref, sb_ref):
    for h in range(RET_HEADS):
        q_h = q_ref[:, h * RET_DK:(h + 1) * RET_DK]
        k_h = k_ref[:, h * RET_DK:(h + 1) * RET_DK]
        v_h = v_ref[:, h * RET_DV:(h + 1) * RET_DV]
        inner = (_dot_nt(q_h, k_h) * a_ref[h]).astype(BF16)
        o = _dot(inner, v_h)
        y_ref[:, h * RET_DV:(h + 1) * RET_DV] = _group_norm_gate(o, g_ref[:, h * RET_DV:(h + 1) * RET_DV])
        sf_ref[h] = _state_update(k_h, zf_ref[h], v_h)
        sb_ref[h] = _state_update(k_h, zb_ref[h], v_h)


def _ret_ctx(proj, tabs, batch):
    c = RET_C
    qw, vw = RET_HEADS * RET_DK, RET_HEADS * RET_DV
    tab_spec = lambda arr: pl.BlockSpec(arr.shape, lambda b: (0, 0, 0))
    st_shape = jax.ShapeDtypeStruct((batch, RET_HEADS, RET_DK, RET_DV), F32)
    st_spec = pl.BlockSpec((None, RET_HEADS, RET_DK, RET_DV), lambda b: (b, 0, 0, 0))
    return pl.pallas_call(
        _ret_ctx_kernel,
        grid=(batch,),
        in_specs=[pl.BlockSpec((c, qw), lambda b: (b, 0)),
                  pl.BlockSpec((c, qw), lambda b: (b, 1)),
                  pl.BlockSpec((c, vw), lambda b: (b, 1)),
                  pl.BlockSpec((c, vw), lambda b: (b, 2)),
                  tab_spec(tabs["amat"]), tab_spec(tabs["zf"]), tab_spec(tabs["zb"])],
        out_specs=[pl.BlockSpec((c, vw), lambda b: (b, 0)), st_spec, st_spec],
        out_shape=[jax.ShapeDtypeStruct((batch * c, vw), BF16), st_shape, st_shape],
        compiler_params=_cparams(1),
        name="retention_context",
    )(proj, proj, proj, proj, tabs["amat"], tabs["zf"], tabs["zb"])


def _ret_bwd_kernel(q_ref, k_ref, v_ref, s0_ref, xib_ref, zb_ref, decb_ref, ob_ref, s_ref):
    @pl.when(pl.program_id(1) == 0)
    def _():
        s_ref[...] = s0_ref[...]

    for h in range(RET_HEADS):
        q_h = q_ref[:, h * RET_DK:(h + 1) * RET_DK]
        k_h = k_ref[:, h * RET_DK:(h + 1) * RET_DK]
        v_h = v_ref[:, h * RET_DV:(h + 1) * RET_DV]
        s = s_ref[h]
        ob_ref[:, h * RET_DV:(h + 1) * RET_DV] = _dot(q_h, s.astype(BF16)) * xib_ref[h]
        s_ref[h] = s * decb_ref[h] + _state_update(k_h, zb_ref[h], v_h)


def _ret_fwd_kernel(q_ref, k_ref, v_ref, g_ref, ob_ref, s0_ref, a_ref, xif_ref, zf_ref, decf_ref, y_ref, s_ref):
    @pl.when(pl.program_id(1) == 0)
    def _():
        s_ref[...] = s0_ref[...]

    for h in range(RET_HEADS):
        q_h = q_ref[:, h * RET_DK:(h + 1) * RET_DK]
        k_h = k_ref[:, h * RET_DK:(h + 1) * RET_DK]
        v_h = v_ref[:, h * RET_DV:(h + 1) * RET_DV]
        s = s_ref[h]
        inner = (_dot_nt(q_h, k_h) * a_ref[h]).astype(BF16)
        o = (_dot(inner, v_h) + _dot(q_h, s.astype(BF16)) * xif_ref[h]) + ob_ref[:, h * RET_DV:(h + 1) * RET_DV]
        y_ref[:, h * RET_DV:(h + 1) * RET_DV] = _group_norm_gate(o, g_ref[:, h * RET_DV:(h + 1) * RET_DV])
        s_ref[h] = s * decf_ref[h] + _state_update(k_h, zf_ref[h], v_h)


def _ret_lat(proj, tabs, s0_f, s0_b, batch, length):
    c = RET_C
    nc = length // c
    qw, vw = RET_HEADS * RET_DK, RET_HEADS * RET_DV
    t = batch * length
    tab_spec = lambda arr: pl.BlockSpec(arr.shape, lambda b, j: (0, 0, 0))
    st_spec = pl.BlockSpec((None, RET_HEADS, RET_DK, RET_DV), lambda b, j: (b, 0, 0, 0))
    st_scratch = pltpu.VMEM((RET_HEADS, RET_DK, RET_DV), F32)
    rev = lambda b, j: b * nc + (nc - 1 - j)
    fwd = lambda b, j: b * nc + j
    ob = pl.pallas_call(
        _ret_bwd_kernel,
        grid=(batch, nc),
        in_specs=[pl.BlockSpec((c, qw), lambda b, j: (rev(b, j), 0)),
                  pl.BlockSpec((c, qw), lambda b, j: (rev(b, j), 1)),
                  pl.BlockSpec((c, vw), lambda b, j: (rev(b, j), 1)),
                  st_spec, tab_spec(tabs["xib"]), tab_spec(tabs["zb"]), tab_spec(tabs["decb"])],
        out_specs=pl.BlockSpec((c, vw), lambda b, j: (rev(b, j), 0)),
        out_shape=jax.ShapeDtypeStruct((t, vw), F32),
        scratch_shapes=[st_scratch],
        compiler_params=_cparams(2),
        name="retention_latent_backward",
    )(proj, proj, proj, s0_b, tabs["xib"], tabs["zb"], tabs["decb"])
    return pl.pallas_call(
        _ret_fwd_kernel,
        grid=(batch, nc),
        in_specs=[pl.BlockSpec((c, qw), lambda b, j: (fwd(b, j), 0)),
                  pl.BlockSpec((c, qw), lambda b, j: (fwd(b, j), 1)),
                  pl.BlockSpec((c, vw), lambda b, j: (fwd(b, j), 1)),
                  pl.BlockSpec((c, vw), lambda b, j: (fwd(b, j), 2)),
                  pl.BlockSpec((c, vw), lambda b, j: (fwd(b, j), 0)),
                  st_spec, tab_spec(tabs["amat"]), tab_spec(tabs["xif"]), tab_spec(tabs["zf"]),
                  tab_spec(tabs["decf"])],
        out_specs=pl.BlockSpec((c, vw), lambda b, j: (fwd(b, j), 0)),
        out_shape=jax.ShapeDtypeStruct((t, vw), BF16),
        scratch_shapes=[st_scratch],
        compiler_params=_cparams(2),
        name="retention_latent_forward",
    )(proj, proj, proj, proj, ob, s0_f, tabs["amat"], tabs["xif"], tabs["zf"], tabs["decf"])


def _attn_ctx_kernel(sink_ref, q_ref, k_ref, v_ref, o_ref):
    scale = HEAD_DIM ** -0.5
    outs = []
    for g in range(ATTN_KV):
        k_g = k_ref[:, g * HEAD_DIM:(g + 1) * HEAD_DIM].astype(BF16)
        v_g = v_ref[:, g * HEAD_DIM:(g + 1) * HEAD_DIM].astype(BF16)
        for r in range(ATTN_REP):
            hh = g * ATTN_REP + r
            q_h = q_ref[:, hh * HEAD_DIM:(hh + 1) * HEAD_DIM].astype(BF16)
            s = _dot_nt(q_h, k_g) * scale
            snk = sink_ref[hh]
            m = jnp.maximum(jnp.max(s, axis=-1, keepdims=True), snk)
            p = jnp.exp(s - m)
            den = jnp.sum(p, axis=-1, keepdims=True) + jnp.exp(snk - m)
            outs.append(_dot(p.astype(BF16), v_g) / den)
    o_ref[...] = jnp.concatenate(outs, axis=1).astype(o_ref.dtype)


def _attn_ctx(qkv, sink, batch, length):
    qw = ATTN_HEADS * HEAD_DIM
    kw = ATTN_KV * HEAD_DIM
    return pl.pallas_call(
        _attn_ctx_kernel,
        grid=(batch,),
        in_specs=[pl.BlockSpec(memory_space=pltpu.SMEM),
                  pl.BlockSpec((length, qw), lambda b: (b, 0)),
                  pl.BlockSpec((length, kw), lambda b: (b, qw // kw)),
                  pl.BlockSpec((length, kw), lambda b: (b, qw // kw + 1))],
        out_specs=pl.BlockSpec((length, qw), lambda b: (b, 0)),
        out_shape=jax.ShapeDtypeStruct((batch * length, qw), BF16),
        compiler_params=_cparams(1),
        name="attention_context",
    )(sink, qkv, qkv, qkv)


def _attn_lat_kernel(sink_ref, q_ref, k_ref, v_ref, kc_ref, vc_ref, o_ref, *, length):
    scale = HEAD_DIM ** -0.5
    band = 3 * QBLK
    j = pl.program_id(1)
    start = pl.multiple_of(jnp.clip((j - 1) * QBLK, 0, length - band), QBLK)
    rows = ATTN_REP * QBLK
    qpos = j * QBLK + lax.broadcasted_iota(I32, (rows, band), 0) % QBLK
    kpos = start + lax.broadcasted_iota(I32, (rows, band), 1)
    valid = jnp.abs(qpos - kpos) <= WINDOW
    rep_of_row = lax.broadcasted_iota(I32, (rows, 1), 0) // QBLK
    kb = k_ref[pl.ds(start, band), :]
    vb = v_ref[pl.ds(start, band), :]
    outs = []
    for g in range(ATTN_KV):
        cs = slice(g * HEAD_DIM, (g + 1) * HEAD_DIM)
        k_g = kb[:, cs].astype(BF16)
        v_g = vb[:, cs].astype(BF16)
        kc_g = kc_ref[:, cs].astype(BF16)
        vc_g = vc_ref[:, cs].astype(BF16)
        q_g = jnp.concatenate(
            [q_ref[:, (g * ATTN_REP + r) * HEAD_DIM:(g * ATTN_REP + r + 1) * HEAD_DIM] for r in range(ATTN_REP)],
            axis=0).astype(BF16)
        snk = jnp.zeros((rows, 1), F32)
        for r in range(ATTN_REP):
            snk = jnp.where(rep_of_row == r, sink_ref[g * ATTN_REP + r], snk)
        s_loc = jnp.where(valid, _dot_nt(q_g, k_g) * scale, NEG_INF)
        s_ctx = _dot_nt(q_g, kc_g) * scale
        m = jnp.maximum(jnp.maximum(jnp.max(s_loc, axis=-1, keepdims=True),
                                    jnp.max(s_ctx, axis=-1, keepdims=True)), snk)
        p_loc = jnp.exp(s_loc - m)
        p_ctx = jnp.exp(s_ctx - m)
        den = (jnp.sum(p_loc, axis=-1, keepdims=True) + jnp.sum(p_ctx, axis=-1, keepdims=True)) + jnp.exp(snk - m)
        o_g = (_dot(p_loc.astype(BF16), v_g) + _dot(p_ctx.astype(BF16), vc_g)) / den
        for r in range(ATTN_REP):
            outs.append(o_g[r * QBLK:(r + 1) * QBLK, :])
    o_ref[...] = jnp.concatenate(outs, axis=1).astype(o_ref.dtype)


def _attn_lat(qkv, sink, kc, vc, batch, length):
    qw = ATTN_HEADS * HEAD_DIM
    kw = ATTN_KV * HEAD_DIM
    nq = length // QBLK
    past = kc.shape[1]
    return pl.pallas_call(
        functools.partial(_attn_lat_kernel, length=length),
        grid=(batch, nq),
        in_specs=[pl.BlockSpec(memory_space=pltpu.SMEM),
                  pl.BlockSpec((QBLK, qw), lambda b, j: (b * nq + j, 0)),
                  pl.BlockSpec((length, kw), lambda b, j: (b, qw // kw)),
                  pl.BlockSpec((length, kw), lambda b, j: (b, qw // kw + 1)),
                  pl.BlockSpec((None, past, kw), lambda b, j: (b, 0, 0)),
                  pl.BlockSpec((None, past, kw), lambda b, j: (b, 0, 0))],
        out_specs=pl.BlockSpec((QBLK, qw), lambda b, j: (b * nq + j, 0)),
        out_shape=jax.ShapeDtypeStruct((batch * length, qw), BF16),
        compiler_params=_cparams(2),
        name="attention_latent",
    )(sink, qkv, qkv, qkv, kc, vc)


def _rope_tables(length):
    axis_dim = HEAD_DIM // 2
    pos = jnp.arange(length)
    row = (pos // GRID_W).astype(F32)
    colp = (pos % GRID_W).astype(F32)
    inv = ROPE_BASE ** (-jnp.arange(0, axis_dim, 2, dtype=F32) / axis_dim)
    ar = row[:, None] * inv[None, :]
    ac = colp[:, None] * inv[None, :]
    cos = jnp.concatenate([jnp.cos(ar), jnp.cos(ar), jnp.cos(ac), jnp.cos(ac)], axis=1)
    sin = jnp.concatenate([-jnp.sin(ar), jnp.sin(ar), -jnp.sin(ac), jnp.sin(ac)], axis=1)
    return jnp.concatenate([cos, cos], axis=1), jnp.concatenate([sin, sin], axis=1)


def _router_kernel(x_ref, g_ref, sh_ref, sc_ref, wr_ref, br_ref, h_ref, e_ref, p1_ref, p2_ref, n_ref, cnt_ref,
                   *, tiles_per_chunk):
    h = _norm_mod(x_ref[...], g_ref[...], sh_ref[...], sc_ref[...])
    _to_slabs(h_ref, h)
    logits = _dot3(h, wr_ref[...]) + br_ref[...]
    lane = lax.broadcasted_iota(I32, logits.shape, 1).astype(F32)
    ninf = -jnp.inf
    lg = jnp.where(lane < N_GROUPS, logits, ninf)
    gmax = jnp.max(lg, axis=-1, keepdims=True)
    gtop = jnp.min(jnp.where(lg == gmax, lane, float(LANE)), axis=-1, keepdims=True)
    pg_sel = 1.0 / jnp.sum(jnp.exp(lg - gmax), axis=-1, keepdims=True)
    lo = N_GROUPS + EXP_PER_GROUP * gtop
    le = jnp.where((lane >= lo) & (lane < lo + EXP_PER_GROUP), logits, ninf)
    v1 = jnp.max(le, axis=-1, keepdims=True)
    i1 = jnp.min(jnp.where(le == v1, lane, float(LANE)), axis=-1, keepdims=True)
    le2 = jnp.where(lane == i1, ninf, le)
    v2 = jnp.max(le2, axis=-1, keepdims=True)
    i2 = jnp.min(jnp.where(le2 == v2, lane, float(LANE)), axis=-1, keepdims=True)
    t = jnp.exp(v2 - v1)
    gate1 = pg_sel * (1.0 / (1.0 + t))
    gate2 = pg_sel * (t / (1.0 + t))
    p1_ref[...] = jnp.broadcast_to(gate1, p1_ref.shape)
    p2_ref[...] = jnp.broadcast_to(gate2, p2_ref.shape)

    e1 = i1 - N_GROUPS
    e2 = i2 - N_GROUPS
    tm = logits.shape[0]

    @pl.when(pl.program_id(0) % tiles_per_chunk == 0)
    def _():
        cnt_ref[...] = jnp.zeros_like(cnt_ref)

    carry = cnt_ref[0:1, :]
    oh1 = (lane == e1).astype(F32)
    oh2 = (lane == e2).astype(F32)
    tri = (lax.broadcasted_iota(I32, (tm, tm), 0) >= lax.broadcasted_iota(I32, (tm, tm), 1)).astype(BF16)
    cs1 = _dot(tri, oh1.astype(BF16))
    cs2 = _dot(tri, oh2.astype(BF16))
    tot1 = cs1[tm - 1:tm, :]
    tot2 = cs2[tm - 1:tm, :]
    rank1 = jnp.sum(oh1 * (carry + cs1 - 1.0), axis=-1, keepdims=True)
    rank2 = jnp.sum(oh2 * ((carry + tot1) + cs2 - 1.0), axis=-1, keepdims=True)
    code1 = rank1 * float(N_EXP) + e1
    code2 = rank2 * float(N_EXP) + e2
    e_ref[...] = jnp.where(lane == 0, code1, jnp.where(lane == 1, code2, 0.0)).astype(I32)
    new_cnt = (carry + tot1) + tot2
    cnt_ref[...] = jnp.broadcast_to(new_cnt, cnt_ref.shape)
    n_ref[...] = jnp.broadcast_to(new_cnt, n_ref.shape).astype(I32)


def _router(x, mods, layer, norm_g, wr, br, is_lat):
    t = x.shape[0]
    tm = 256
    seg_fn = _seg_of(is_lat, tm)
    tiles_per_chunk = SEG // tm
    return pl.pallas_call(
        functools.partial(_router_kernel, tiles_per_chunk=tiles_per_chunk),
        grid=(t // tm,),
        in_specs=[pl.BlockSpec((tm, D), lambda i: (i, 0)),
                  pl.BlockSpec((1, D), lambda i: (0, 0)),
                  _mod_spec(layer, 3, seg_fn),
                  _mod_spec(layer, 4, seg_fn),
                  pl.BlockSpec((D, LANE), lambda i: (0, 0)),
                  pl.BlockSpec((1, LANE), lambda i: (0, 0))],
        out_specs=[pl.BlockSpec((tm * SUB, LANE), lambda i: (i, 0)),
                   pl.BlockSpec((tm, LANE), lambda i: (i, 0)),
                   pl.BlockSpec((tm, LANE), lambda i: (i, 0)),
                   pl.BlockSpec((tm, LANE), lambda i: (i, 0)),
                   pl.BlockSpec((SUB, LANE), lambda i: (i // tiles_per_chunk, 0))],
        out_shape=[jax.ShapeDtypeStruct((t * SUB, LANE), F32),
                   jax.ShapeDtypeStruct((t, LANE), I32),
                   jax.ShapeDtypeStruct((t, LANE), F32),
                   jax.ShapeDtypeStruct((t, LANE), F32),
                   jax.ShapeDtypeStruct((t // SEG * SUB, LANE), I32)],
        scratch_shapes=[pltpu.VMEM((SUB, LANE), F32)],
        compiler_params=_cparams(1),
        name="moe_router",
    )(x, norm_g.reshape(1, D), mods, mods, wr, br)


def _moe_kernel(dest_ref, meta_ref, h_hbm, g1_hbm, g2_hbm, w1_ref, w3_ref, w2_ref, y_hbm,
                h_vm, y_vm, g_vm, xt, ot, gt, w1b, w3b, w2b, slot_tab, sem, *, tc, a_pad):
    chunk = pl.program_id(0)
    e = pl.program_id(1)
    stride = MOE_TILE_STRIDE
    n_lane_chunks = D // LANE
    h_rows = tc * SUB
    n_assign = 2 * tc
    slab_mask = (1 << 16) - 1

    def h_copy():
        return pltpu.make_async_copy(h_hbm.at[pl.ds(chunk * h_rows, h_rows)], h_vm.at[pl.ds(0, h_rows)], sem.at[0])

    def y_copy():
        return pltpu.make_async_copy(y_vm.at[pl.ds(0, h_rows)], y_hbm.at[pl.ds(chunk * h_rows, h_rows)], sem.at[1])

    def g_copy(k, g_hbm):
        return pltpu.make_async_copy(g_hbm.at[pl.ds(chunk * tc, tc)], g_vm.at[pl.ds(k * tc, tc)], sem.at[2 + k])

    @pl.when(e == 0)
    def _():
        h_copy().start()
        g_copy(0, g1_hbm).start()
        g_copy(1, g2_hbm).start()

        def zero(i, carry):
            y_vm[pl.ds(pl.multiple_of(i * 512, 512), 512), :] = jnp.zeros((512, LANE), F32)
            return carry

        lax.fori_loop(0, h_rows // 512, zero, 0)
        y_vm[pl.ds(h_rows, SUB), :] = jnp.zeros((SUB, LANE), F32)
        h_vm[pl.ds(h_rows, SUB), :] = jnp.zeros((SUB, LANE), F32)

        meta_base = chunk * 3 * N_EXP

        def fill_run(ex, carry):
            run_start = meta_ref[meta_base + ex]

            def fill(s, c2):
                slot_tab[run_start + s] = h_rows
                return c2

            return lax.fori_loop(meta_ref[meta_base + 2 * N_EXP + ex], meta_ref[meta_base + N_EXP + ex], fill, carry)

        lax.fori_loop(0, N_EXP, fill_run, 0)

        tok_step = SUB + (1 << 16)

        def invert(i, carry):
            entry0 = i * ((SUB // 2) * tok_step)
            code0 = chunk * n_assign + i * SUB
            for u in range(SUB):
                code = dest_ref[code0 + u]
                slot = meta_ref[meta_base + (code & (N_EXP - 1))] + lax.shift_right_logical(code, 5)
                slot_tab[slot] = entry0 + ((u // 2) * tok_step + (u % 2) * (tc << 16))
            return carry

        lax.fori_loop(0, n_assign // SUB, invert, 0)
        h_copy().wait()
        g_copy(0, g1_hbm).wait()
        g_copy(1, g2_hbm).wait()

    w1b[...] = w1_ref[...].astype(BF16)
    w3b[...] = w3_ref[...].astype(BF16)
    w2b[...] = w2_ref[...].astype(BF16)

    start = meta_ref[chunk * 3 * N_EXP + e]
    n_pad = meta_ref[chunk * 3 * N_EXP + N_EXP + e]

    def process(off, m):
        off = pl.multiple_of(off, MOE_PAD)

        def gather(i, carry):
            for u in range(SUB):
                r = i * SUB + u
                entry = slot_tab[off + r]
                t8 = pl.multiple_of(entry & slab_mask, SUB)
                xt[pl.ds(r, SUB, stride=stride), :] = h_vm[pl.ds(t8, SUB), :]
                gt[pl.ds(r, 1), :] = g_vm[pl.ds(lax.shift_right_logical(entry, 16), 1), :]
            return carry

        lax.fori_loop(0, m // SUB, gather, 0)
        x = jnp.concatenate([xt[pl.ds(c * stride, m), :] for c in range(n_lane_chunks)], axis=1).astype(BF16)
        a = _dot(x, w1b[...])
        b = _dot(x, w3b[...])
        hm = (_silu(a) * b).astype(BF16)
        out = _dot(hm, w2b[...])
        gate = gt[pl.ds(0, m), :]
        for c in range(n_lane_chunks):
            ot[pl.ds(c * stride, m), :] = out[:, c * LANE:(c + 1) * LANE] * gate

        def scatter(i, carry):
            t8s, news = [], []
            for u in range(4):
                r = i * 4 + u
                t8 = pl.multiple_of(slot_tab[off + r] & slab_mask, SUB)
                t8s.append(t8)
                news.append(y_vm[pl.ds(t8, SUB), :] + ot[pl.ds(r, SUB, stride=stride), :])
            for u in range(4):
                y_vm[pl.ds(t8s[u], SUB), :] = news[u]
            return carry

        lax.fori_loop(0, m // 4, scatter, 0)

    n_full = n_pad // 256

    def full_body(i, carry):
        process(start + i * 256, 256)
        return carry

    lax.fori_loop(0, n_full, full_body, 0)
    rem = start + n_full * 256

    @pl.when((n_pad & 128) != 0)
    def _():
        process(rem, 128)

    @pl.when((n_pad & 64) != 0)
    def _():
        process(rem + (n_pad & 128), 64)

    @pl.when(e == N_EXP - 1)
    def _():
        y_copy().start()
        y_copy().wait()


def _moe(h_slabs, codes, counts, gate1, gate2, w1, w3, w2, layer):
    t = codes.shape[0]
    tc = SEG
    n_chunks = t // tc
    a_pad = 2 * tc + N_EXP * MOE_PAD
    padded = (counts + MOE_PAD - 1) // MOE_PAD * MOE_PAD
    meta = jnp.concatenate([jnp.cumsum(padded, axis=1) - padded, padded, counts], axis=1).astype(I32)
    dest = codes
    slab_rows = (tc + 1) * SUB
    tile_rows = (D // LANE) * MOE_TILE_STRIDE
    hbm_spec = pl.BlockSpec(memory_space=pl.ANY)
    grid_spec = pltpu.PrefetchScalarGridSpec(
        num_scalar_prefetch=2,
        grid=(n_chunks, N_EXP),
        in_specs=[hbm_spec, hbm_spec, hbm_spec,
                  pl.BlockSpec((None, None, D, D_FF), lambda c, e, *_: (layer, e, 0, 0)),
                  pl.BlockSpec((None, None, D, D_FF), lambda c, e, *_: (layer, e, 0, 0)),
                  pl.BlockSpec((None, None, D_FF, D), lambda c, e, *_: (layer, e, 0, 0))],
        out_specs=pl.BlockSpec(memory_space=pl.ANY),
        scratch_shapes=[pltpu.VMEM((slab_rows, LANE), F32),
                        pltpu.VMEM((slab_rows, LANE), F32),
                        pltpu.VMEM((2 * tc, LANE), F32),
                        pltpu.VMEM((tile_rows, LANE), F32),
                        pltpu.VMEM((tile_rows, LANE), F32),
                        pltpu.VMEM((256, LANE), F32),
                        pltpu.VMEM((D, D_FF), BF16),
                        pltpu.VMEM((D, D_FF), BF16),
                        pltpu.VMEM((D_FF, D), BF16),
                        pltpu.SMEM((a_pad,), I32),
                        pltpu.SemaphoreType.DMA((4,))],
    )
    return pl.pallas_call(
        functools.partial(_moe_kernel, tc=tc, a_pad=a_pad),
        grid_spec=grid_spec,
        out_shape=jax.ShapeDtypeStruct((t * SUB, LANE), F32),
        compiler_params=pltpu.CompilerParams(dimension_semantics=("arbitrary", "arbitrary"),
                                             vmem_limit_bytes=V7X_VMEM_LIMIT_MOE),
        name="moe_experts",
    )(dest.reshape(n_chunks * 2 * tc), meta.reshape(n_chunks * 3 * N_EXP), h_slabs, gate1, gate2, w1, w3, w2)


def kernel(x_prompt, x_sample, c, cache_attn_k, cache_attn_v, state_ret_fwd, state_ret_bwd, c_ctx, adaln_w, adaln_b, norm1_g, norm2_g, final_g, fnet_w, ret_w_in, ret_w_out, ret_decay_fwd, ret_decay_bwd, attn_w_in, attn_w_out, attn_sink, router_group_w, router_group_b, router_expert_w, router_expert_b, moe_w1, moe_w3, moe_w2):
    bp, lp, _ = x_prompt.shape
    bs, ls, _ = x_sample.shape
    xs = {False: x_prompt.reshape(bp * lp, D), True: x_sample.reshape(bs * ls, D)}
    dims = {False: (bp, lp), True: (bs, ls)}

    cond8 = jnp.concatenate([c_ctx[None, :], c, jnp.zeros((SUB - 1 - bs, D), F32)], axis=0)
    mods = _adaln(cond8, adaln_w, adaln_b).reshape(DEPTH, SUB, 6, 1, D)

    rope_lat = _rope_tables(ls)
    rope = {False: (jnp.ones((bp * lp, LANE), F32), jnp.zeros((bp * lp, LANE), F32)),
            True: (jnp.tile(rope_lat[0], (bs, 1)), jnp.tile(rope_lat[1], (bs, 1)))}

    extras = {}
    for i in range(DEPTH):
        kind, j = i % 3, i // 3
        if kind == 0:
            lw = dict(w_fold=_fnet_fold(fnet_w[j]))
        elif kind == 1:
            lw = dict(w_in=ret_w_in[j].astype(BF16), w_out=ret_w_out[j].astype(BF16),
                      tabs=_ret_tables(ret_decay_fwd[j], ret_decay_bwd[j]),
                      s0_f=state_ret_fwd[:, j], s0_b=state_ret_bwd[:, j])
        else:
            kvw = ATTN_KV * HEAD_DIM
            lw = dict(w_in=attn_w_in[j].astype(BF16), w_out=attn_w_out[j].astype(BF16), sink=attn_sink[j],
                      kc=cache_attn_k[:, j].reshape(bs, -1, kvw), vc=cache_attn_v[:, j].reshape(bs, -1, kvw))
        wr = jnp.concatenate([router_group_w[i], jnp.transpose(router_expert_w[i], (1, 0, 2)).reshape(D, N_EXP),
                              jnp.zeros((D, LANE - N_GROUPS - N_EXP), F32)], axis=1)
        br = jnp.concatenate([router_group_b[i], router_expert_b[i].reshape(N_EXP),
                              jnp.zeros((LANE - N_GROUPS - N_EXP,), F32)]).reshape(1, LANE)

        for is_lat in (False, True):
            b_, l_ = dims[is_lat]
            x = _mixer(i, xs[is_lat], mods, norm1_g[i], lw, is_lat, b_, l_, rope[is_lat], extras)
            xs[is_lat] = _ffn(i, x, mods, norm2_g[i], wr, br, moe_w1, moe_w3, moe_w2, is_lat)

    y_prompt = _final_norm(xs[False], final_g).reshape(bp, lp, D)
    y_sample = _final_norm(xs[True], final_g).reshape(bs, ls, D)
    return (y_prompt, y_sample, extras["k"], extras["v"], extras["sf"], extras["sb"])


def _mixer(i, x, mods, norm_g, lw, is_lat, b_, l_, rope, extras):
    kind = i % 3
    if kind == 0:
        y = _proj(x, mods, i, 0, norm_g, lw["w_fold"], is_lat, BF16)
        return _seqmix(y, x, mods, i, is_lat, b_, l_)
    if kind == 1:
        proj = _proj(x, mods, i, 0, norm_g, lw["w_in"], is_lat, BF16)
        if is_lat:
            yr = _ret_lat(proj, lw["tabs"], lw["s0_f"], lw["s0_b"], b_, l_)
        else:
            yr, sf, sb = _ret_ctx(proj, lw["tabs"], b_)
            extras["sf"], extras["sb"] = sf[:, None], sb[:, None]
        return _linres(yr, lw["w_out"], x, mods, i, 2, is_lat)
    qkv = _proj(x, mods, i, 0, norm_g, lw["w_in"], is_lat, F32, rope=rope)
    if is_lat:
        o = _attn_lat(qkv, lw["sink"], lw["kc"], lw["vc"], b_, l_)
    else:
        qw = ATTN_HEADS * HEAD_DIM
        kw = ATTN_KV * HEAD_DIM
        extras["k"] = qkv[:, qw:qw + kw].reshape(b_, 1, l_, ATTN_KV, HEAD_DIM)
        extras["v"] = qkv[:, qw + kw:].reshape(b_, 1, l_, ATTN_KV, HEAD_DIM)
        o = _attn_ctx(qkv, lw["sink"], b_, l_)
    return _linres(o, lw["w_out"], x, mods, i, 2, is_lat)


def _ffn(i, x, mods, norm_g, wr, br, w1, w3, w2, is_lat):
    h_slabs, codes, gate1, gate2, counts = _router(x, mods, i, norm_g, wr, br, is_lat)
    counts = counts.reshape(-1, SUB, LANE)[:, 0, :N_EXP]
    y_slabs = _moe(h_slabs, codes[:, :2], counts, gate1, gate2, w1, w3, w2, i)
    return _residual(x, y_slabs, mods, i, 5, is_lat)
```

```python
import functools
import math

import jax
import jax.numpy as jnp
from jax import lax
from jax.experimental import pallas as pl
from jax.experimental.pallas import tpu as pltpu

F32 = jnp.float32
BF16 = jnp.bfloat16
I32 = jnp.int32

D = 1024
DEPTH = 4
SEG = 4096
FNET_GROUP_DIM = 256
RET_HEADS, RET_DK, RET_DV = 4, 256, 512
RET_C = 256
ATTN_HEADS, ATTN_KV, HEAD_DIM, ATTN_REP = 16, 4, 64, 4
WINDOW, QBLK = 128, 128
GRID_W = 64
ROPE_BASE = 10000.0
N_EXP, EXP_PER_GROUP, N_GROUPS, D_FF = 32, 8, 4, 512
EPS = 1e-6
NEG_INF = -1e30
LANE = 128
SUB = 8

V7X_VMEM_LIMIT = 56 * 1024 * 1024
V7X_VMEM_LIMIT_MOE = 59 * 1024 * 1024
MOE_PAD = 64
MOE_TILE_STRIDE = 264


def _cparams(n_axes):
    return pltpu.CompilerParams(dimension_semantics=("arbitrary",) * n_axes, vmem_limit_bytes=V7X_VMEM_LIMIT)


def _dot(a, b):
    return jnp.dot(a, b, preferred_element_type=F32)


def _dot_nt(a, b):
    return lax.dot_general(a, b, (((1,), (1,)), ((), ())), preferred_element_type=F32)


def _split(x):
    hi = x.astype(BF16)
    lo = (x - hi.astype(F32)).astype(BF16)
    return hi, lo


def _dot3(a, b):
    ah, al = _split(a)
    bh, bl = _split(b)
    return _dot(ah, bh) + (_dot(ah, bl) + _dot(al, bh))


def _silu(x):
    return x * jax.nn.sigmoid(x)


def _norm_mod(x, g, sh, sc):
    ms = jnp.mean(x * x, axis=-1, keepdims=True)
    return (x * lax.rsqrt(ms + EPS) * g) * (1.0 + sc) + sh


def _seg_of(is_lat, tm):
    if is_lat:
        return lambda i: 1 + (i * tm) // SEG
    return lambda i: 0


def _mod_spec(layer, kind, seg_fn):
    return pl.BlockSpec((None, None, None, 1, D), lambda i, *_: (layer, seg_fn(i), kind, 0, 0))


def _adaln_kernel(c_ref, w_ref, b_ref, o_ref):
    o_ref[...] = _dot3(_silu(c_ref[...]), w_ref[...]) + b_ref[...]


def _adaln(cond8, w, b):
    depth, _, n = w.shape
    tn = 1536
    return pl.pallas_call(
        _adaln_kernel,
        grid=(depth, n // tn),
        in_specs=[pl.BlockSpec((SUB, D), lambda l, j: (0, 0)),
                  pl.BlockSpec((None, D, tn), lambda l, j: (l, 0, j)),
                  pl.BlockSpec((None, 1, tn), lambda l, j: (l, 0, j))],
        out_specs=pl.BlockSpec((None, SUB, tn), lambda l, j: (l, 0, j)),
        out_shape=jax.ShapeDtypeStruct((depth, SUB, n), F32),
        compiler_params=_cparams(2),
        name="adaln",
    )(cond8, w, b.reshape(depth, 1, n))


def _rope(x, cos_t, sin_t):
    w = x.shape[1]
    reps = w // LANE
    c = jnp.concatenate([cos_t] * reps, axis=1)
    s = jnp.concatenate([sin_t] * reps, axis=1)
    lane = lax.broadcasted_iota(I32, x.shape, 1)
    first = (lane % 32) < 16
    partner = jnp.where(first, pltpu.roll(x, w - 16, 1), pltpu.roll(x, 16, 1))
    return x * c + partner * s


def _proj_kernel(x_ref, g_ref, sh_ref, sc_ref, w_ref, *rest, n_rope):
    if n_rope:
        cos_ref, sin_ref, o_ref = rest
    else:
        (o_ref,) = rest
    hb = _norm_mod(x_ref[...], g_ref[...], sh_ref[...], sc_ref[...]).astype(BF16)
    n = w_ref.shape[1]
    tn = 512
    if n_rope:
        q_w = ATTN_HEADS * HEAD_DIM
        k_w = ATTN_KV * HEAD_DIM
        q = _dot(hb, w_ref[:, :q_w])
        o_ref[:, :q_w] = _rope(q, cos_ref[...], sin_ref[...])
        k = _dot(hb, w_ref[:, q_w:q_w + k_w])
        o_ref[:, q_w:q_w + k_w] = _rope(k, cos_ref[...], sin_ref[...])
        o_ref[:, q_w + k_w:] = _dot(hb, w_ref[:, q_w + k_w:])
    else:
        for j in range(n // tn):
            o_ref[:, j * tn:(j + 1) * tn] = _dot(hb, w_ref[:, j * tn:(j + 1) * tn]).astype(o_ref.dtype)


def _proj(x, mods, layer, sub, norm_g, w_bf16, is_lat, out_dtype, rope=None):
    t = x.shape[0]
    n = w_bf16.shape[1]
    tm = 256
    seg_fn = _seg_of(is_lat, tm)
    in_specs = [pl.BlockSpec((tm, D), lambda i: (i, 0)),
                pl.BlockSpec((1, D), lambda i: (0, 0)),
                _mod_spec(layer, 3 * sub + 0, seg_fn),
                _mod_spec(layer, 3 * sub + 1, seg_fn),
                pl.BlockSpec((D, n), lambda i: (0, 0))]
    args = [x, norm_g.reshape(1, D), mods, mods, w_bf16]
    if rope is not None:
        in_specs += [pl.BlockSpec((tm, LANE), lambda i: (i, 0))] * 2
        args += list(rope)
    return pl.pallas_call(
        functools.partial(_proj_kernel, n_rope=rope is not None),
        grid=(t // tm,),
        in_specs=in_specs,
        out_specs=pl.BlockSpec((tm, n), lambda i: (i, 0)),
        out_shape=jax.ShapeDtypeStruct((t, n), out_dtype),
        compiler_params=_cparams(1),
        name="norm_mod_proj",
    )(*args)


def _linres_kernel(y_ref, w_ref, x_ref, g_ref, o_ref):
    o_ref[...] = x_ref[...] + g_ref[...] * _dot(y_ref[...].astype(BF16), w_ref[...])


def _linres(y, w_bf16, x, mods, layer, kind, is_lat):
    t, k = y.shape
    tm = 256
    seg_fn = _seg_of(is_lat, tm)
    return pl.pallas_call(
        _linres_kernel,
        grid=(t // tm,),
        in_specs=[pl.BlockSpec((tm, k), lambda i: (i, 0)),
                  pl.BlockSpec((k, D), lambda i: (0, 0)),
                  pl.BlockSpec((tm, D), lambda i: (i, 0)),
                  _mod_spec(layer, kind, seg_fn)],
        out_specs=pl.BlockSpec((tm, D), lambda i: (i, 0)),
        out_shape=jax.ShapeDtypeStruct((t, D), F32),
        compiler_params=_cparams(1),
        name="linear_residual",
    )(y, w_bf16, x, mods)


def _to_slabs(slab_ref, x):
    tm = x.shape[0]
    for c in range(D // LANE):
        slab_ref[pl.ds(c, tm, stride=SUB), :] = x[:, c * LANE:(c + 1) * LANE]


def _from_slabs(slab_ref, tm):
    return jnp.concatenate([slab_ref[pl.ds(c, tm, stride=SUB), :] for c in range(D // LANE)], axis=1)


def _residual_kernel(x_ref, y_ref, g_ref, o_ref):
    o_ref[...] = x_ref[...] + g_ref[...] * _from_slabs(y_ref, x_ref.shape[0])


def _residual(x, y_slabs, mods, layer, kind, is_lat):
    t = x.shape[0]
    tm = 512
    seg_fn = _seg_of(is_lat, tm)
    return pl.pallas_call(
        _residual_kernel,
        grid=(t // tm,),
        in_specs=[pl.BlockSpec((tm, D), lambda i: (i, 0)),
                  pl.BlockSpec((tm * SUB, LANE), lambda i: (i, 0)),
                  _mod_spec(layer, kind, seg_fn)],
        out_specs=pl.BlockSpec((tm, D), lambda i: (i, 0)),
        out_shape=jax.ShapeDtypeStruct((t, D), F32),
        compiler_params=_cparams(1),
        name="gated_residual",
    )(x, y_slabs, mods)


def _final_norm_kernel(x_ref, g_ref, o_ref):
    x = x_ref[...]
    ms = jnp.mean(x * x, axis=-1, keepdims=True)
    o_ref[...] = x * lax.rsqrt(ms + EPS) * g_ref[...]


def _final_norm(x, g):
    t = x.shape[0]
    tm = 512
    return pl.pallas_call(
        _final_norm_kernel,
        grid=(t // tm,),
        in_specs=[pl.BlockSpec((tm, D), lambda i: (i, 0)), pl.BlockSpec((1, D), lambda i: (0, 0))],
        out_specs=pl.BlockSpec((tm, D), lambda i: (i, 0)),
        out_shape=jax.ShapeDtypeStruct((t, D), F32),
        compiler_params=_cparams(1),
        name="final_rmsnorm",
    )(x, g.reshape(1, D))


def _fold_kernel(c_ref, s_ref, w_ref, o_ref):
    w = w_ref[...]
    o_ref[:, :D] = _dot3(c_ref[...], w).astype(BF16)
    o_ref[:, D:] = _dot3(s_ref[...], w).astype(BF16)


def _fnet_fold(w):
    n = jnp.arange(FNET_GROUP_DIM, dtype=I32)
    ang = ((n[:, None] * n[None, :]) % FNET_GROUP_DIM).astype(F32) * (2.0 * math.pi / FNET_GROUP_DIM)
    sc = 1.0 / math.sqrt(FNET_GROUP_DIM)
    gd = FNET_GROUP_DIM
    return pl.pallas_call(
        _fold_kernel,
        grid=(D // gd,),
        in_specs=[pl.BlockSpec((gd, gd), lambda g: (0, 0)),
                  pl.BlockSpec((gd, gd), lambda g: (0, 0)),
                  pl.BlockSpec((gd, D), lambda g: (g, 0))],
        out_specs=pl.BlockSpec((gd, 2 * D), lambda g: (g, 0)),
        out_shape=jax.ShapeDtypeStruct((D, 2 * D), BF16),
        compiler_params=_cparams(1),
        name="fnet_fold_channel_dft",
    )(jnp.cos(ang) * sc, jnp.sin(ang) * sc, w)


def _seq_dft(length):
    n = jnp.arange(length, dtype=I32)
    sc = 1.0 / math.sqrt(length)
    split = 64
    if length <= split * 4:
        ang = ((n[:, None] * n[None, :]) % length).astype(F32) * (2.0 * math.pi / length)
        return (jnp.cos(ang) * sc).astype(BF16), (jnp.sin(ang) * (-sc)).astype(BF16)
    hi = jnp.arange(length // split, dtype=I32)
    lo = jnp.arange(split, dtype=I32)
    ang_a = ((hi[:, None] * n[None, :]) % (length // split)).astype(F32) * (2.0 * math.pi * split / length)
    ang_b = ((lo[:, None] * n[None, :]) % length).astype(F32) * (2.0 * math.pi / length)
    ca, sa = jnp.cos(ang_a)[:, None, :], jnp.sin(ang_a)[:, None, :]
    cb, sb = (jnp.cos(ang_b) * sc)[None, :, :], (jnp.sin(ang_b) * sc)[None, :, :]
    cm = (ca * cb - sa * sb).reshape(length, length)
    sm = (sa * cb + ca * sb).reshape(length, length)
    return cm.astype(BF16), (-sm).astype(BF16)


def _seqmix_kernel(c_ref, s_ref, y_ref, x_ref, g_ref, o_ref, acc_ref, *, nk):
    k = pl.program_id(2)

    @pl.when(k == 0)
    def _():
        acc_ref[...] = jnp.zeros_like(acc_ref)

    acc_ref[...] += _dot(c_ref[...], y_ref[:, :D]) + _dot(s_ref[...], y_ref[:, D:])

    @pl.when(k == nk - 1)
    def _():
        o_ref[...] = x_ref[...] + g_ref[...] * acc_ref[...]


def _seqmix(y, x, mods, layer, is_lat, batch, length):
    cm, sm = _seq_dft(length)
    tm = min(length, 1024)
    tk = min(length, 1024)
    nm, nk = length // tm, length // tk
    seg = (lambda b: 1 + b) if is_lat else (lambda b: 0)
    out = pl.pallas_call(
        functools.partial(_seqmix_kernel, nk=nk),
        grid=(batch, nm, nk),
        in_specs=[pl.BlockSpec((tm, tk), lambda b, m, k: (m, k)),
                  pl.BlockSpec((tm, tk), lambda b, m, k: (m, k)),
                  pl.BlockSpec((None, tk, 2 * D), lambda b, m, k: (b, k, 0)),
                  pl.BlockSpec((None, tm, D), lambda b, m, k: (b, m, 0)),
                  pl.BlockSpec((None, None, None, 1, D), lambda b, m, k: (layer, seg(b), 2, 0, 0))],
        out_specs=pl.BlockSpec((None, tm, D), lambda b, m, k: (b, m, 0)),
        out_shape=jax.ShapeDtypeStruct((batch, length, D), F32),
        scratch_shapes=[pltpu.VMEM((tm, D), F32)],
        compiler_params=_cparams(3),
        name="fnet_seq_dft",
    )(cm, sm, y.reshape(batch, length, 2 * D), x.reshape(batch, length, D), mods)
    return out.reshape(batch * length, D)


def _ret_tables(dec_f, dec_b):
    c = RET_C
    lgf = jax.nn.log_sigmoid(dec_f.astype(F32))[:, None, None]
    lgb = jax.nn.log_sigmoid(dec_b.astype(F32))[:, None, None]
    idx = jnp.arange(c, dtype=F32)
    diff = idx[:, None] - idx[None, :]
    kscale = RET_DK ** -0.5
    amat = jnp.where(diff >= 0, jnp.exp(jnp.maximum(diff, 0.0)[None] * lgf),
                     jnp.exp(jnp.maximum(-diff, 0.0)[None] * lgb)) * kscale
    col = idx[None, :, None]
    ones_k = jnp.ones((1, 1, RET_DK), F32)
    ones_v = jnp.ones((1, 1, RET_DV), F32)
    return dict(
        amat=amat,
        zf=jnp.exp((c - 1.0 - col) * lgf) * kscale * ones_k,
        zb=jnp.exp(col * lgb) * kscale * ones_k,
        xif=jnp.exp((col + 1.0) * lgf) * ones_v,
        xib=jnp.exp((c - col) * lgb) * ones_v,
        decf=jnp.exp(c * lgf) * ones_v,
        decb=jnp.exp(c * lgb) * ones_v,
    )


def _state_update(k_h, z_h, v_h):
    kz_t = jnp.transpose(k_h.astype(F32) * z_h).astype(BF16)
    return _dot(kz_t, v_h)


def _group_norm_gate(o, g_h):
    mu = jnp.mean(o, axis=-1, keepdims=True)
    d = o - mu
    var = jnp.mean(d * d, axis=-1, keepdims=True)
    g = g_h.astype(F32)
    return (_silu(g) * (d * lax.rsqrt(var + EPS))).astype(BF16)


def _ret_ctx_kernel(q_ref, k_ref, v_ref, g_ref, a_ref, zf_ref, zb_ref, y_ref, sf_ref, sb_ref):
    for h in range(RET_HEADS):
        q_h = q_ref[:, h * RET_DK:(h + 1) * RET_DK]
        k_h = k_ref[:, h * RET_DK:(h + 1) * RET_DK]
        v_h = v_ref[:, h * RET_DV:(h + 1) * RET_DV]
        inner = (_dot_nt(q_h, k_h) * a_ref[h]).astype(BF16)
        o = _dot(inner, v_h)
        y_ref[:, h * RET_DV:(h + 1) * RET_DV] = _group_norm_gate(o, g_ref[:, h * RET_DV:(h + 1) * RET_DV])
        sf_ref[h] = _state_update(k_h, zf_ref[h], v_h)
        sb_ref[h] = _state_update(k_h, zb_ref[h], v_h)


def _ret_ctx(proj, tabs, batch):
    c = RET_C
    qw, vw = RET_HEADS * RET_DK, RET_HEADS * RET_DV
    tab_spec = lambda arr: pl.BlockSpec(arr.shape, lambda b: (0, 0, 0))
    st_shape = jax.ShapeDtypeStruct((batch, RET_HEADS, RET_DK, RET_DV), F32)
    st_spec = pl.BlockSpec((None, RET_HEADS, RET_DK, RET_DV), lambda b: (b, 0, 0, 0))
    return pl.pallas_call(
        _ret_ctx_kernel,
        grid=(batch,),
        in_specs=[pl.BlockSpec((c, qw), lambda b: (b, 0)),
                  pl.BlockSpec((c, qw), lambda b: (b, 1)),
                  pl.BlockSpec((c, vw), lambda b: (b, 1)),
                  pl.BlockSpec((c, vw), lambda b: (b, 2)),
                  tab_spec(tabs["amat"]), tab_spec(tabs["zf"]), tab_spec(tabs["zb"])],
        out_specs=[pl.BlockSpec((c, vw), lambda b: (b, 0)), st_spec, st_spec],
        out_shape=[jax.ShapeDtypeStruct((batch * c, vw), BF16), st_shape, st_shape],
        compiler_params=_cparams(1),
        name="retention_context",
    )(proj, proj, proj, proj, tabs["amat"], tabs["zf"], tabs["zb"])


def _ret_bwd_kernel(q_ref, k_ref, v_ref, s0_ref, xib_ref, zb_ref, decb_ref, ob_ref, s_ref):
    @pl.when(pl.program_id(1) == 0)
    def _():
        s_ref[...] = s0_ref[...]

    for h in range(RET_HEADS):
        q_h = q_ref[:, h * RET_DK:(h + 1) * RET_DK]
        k_h = k_ref[:, h * RET_DK:(h + 1) * RET_DK]
        v_h = v_ref[:, h * RET_DV:(h + 1) * RET_DV]
        s = s_ref[h]
        ob_ref[:, h * RET_DV:(h + 1) * RET_DV] = _dot(q_h, s.astype(BF16)) * xib_ref[h]
        s_ref[h] = s * decb_ref[h] + _state_update(k_h, zb_ref[h], v_h)


def _ret_fwd_kernel(q_ref, k_ref, v_ref, g_ref, ob_ref, s0_ref, a_ref, xif_ref, zf_ref, decf_ref, y_ref, s_ref):
    @pl.when(pl.program_id(1) == 0)
    def _():
        s_ref[...] = s0_ref[...]

    for h in range(RET_HEADS):
        q_h = q_ref[:, h * RET_DK:(h + 1) * RET_DK]
        k_h = k_ref[:, h * RET_DK:(h + 1) * RET_DK]
        v_h = v_ref[:, h * RET_DV:(h + 1) * RET_DV]
        s = s_ref[h]
        inner = (_dot_nt(q_h, k_h) * a_ref[h]).astype(BF16)
        o = (_dot(inner, v_h) + _dot(q_h, s.astype(BF16)) * xif_ref[h]) + ob_ref[:, h * RET_DV:(h + 1) * RET_DV]
        y_ref[:, h * RET_DV:(h + 1) * RET_DV] = _group_norm_gate(o, g_ref[:, h * RET_DV:(h + 1) * RET_DV])
        s_ref[h] = s * decf_ref[h] + _state_update(k_h, zf_ref[h], v_h)


def _ret_lat(proj, tabs, s0_f, s0_b, batch, length):
    c = RET_C
    nc = length // c
    qw, vw = RET_HEADS * RET_DK, RET_HEADS * RET_DV
    t = batch * length
    tab_spec = lambda arr: pl.BlockSpec(arr.shape, lambda b, j: (0, 0, 0))
    st_spec = pl.BlockSpec((None, RET_HEADS, RET_DK, RET_DV), lambda b, j: (b, 0, 0, 0))
    st_scratch = pltpu.VMEM((RET_HEADS, RET_DK, RET_DV), F32)
    rev = lambda b, j: b * nc + (nc - 1 - j)
    fwd = lambda b, j: b * nc + j
    ob = pl.pallas_call(
        _ret_bwd_kernel,
        grid=(batch, nc),
        in_specs=[pl.BlockSpec((c, qw), lambda b, j: (rev(b, j), 0)),
                  pl.BlockSpec((c, qw), lambda b, j: (rev(b, j), 1)),
                  pl.BlockSpec((c, vw), lambda b, j: (rev(b, j), 1)),
                  st_spec, tab_spec(tabs["xib"]), tab_spec(tabs["zb"]), tab_spec(tabs["decb"])],
        out_specs=pl.BlockSpec((c, vw), lambda b, j: (rev(b, j), 0)),
        out_shape=jax.ShapeDtypeStruct((t, vw), F32),
        scratch_shapes=[st_scratch],
        compiler_params=_cparams(2),
        name="retention_latent_backward",
    )(proj, proj, proj, s0_b, tabs["xib"], tabs["zb"], tabs["decb"])
    return pl.pallas_call(
        _ret_fwd_kernel,
        grid=(batch, nc),
        in_specs=[pl.BlockSpec((c, qw), lambda b, j: (fwd(b, j), 0)),
                  pl.BlockSpec((c, qw), lambda b, j: (fwd(b, j), 1)),
                  pl.BlockSpec((c, vw), lambda b, j: (fwd(b, j), 1)),
                  pl.BlockSpec((c, vw), lambda b, j: (fwd(b, j), 2)),
                  pl.BlockSpec((c, vw), lambda b, j: (fwd(b, j), 0)),
                  st_spec, tab_spec(tabs["amat"]), tab_spec(tabs["xif"]), tab_spec(tabs["zf"]),
                  tab_spec(tabs["decf"])],
        out_specs=pl.BlockSpec((c, vw), lambda b, j: (fwd(b, j), 0)),
        out_shape=jax.ShapeDtypeStruct((t, vw), BF16),
        scratch_shapes=[st_scratch],
        compiler_params=_cparams(2),
        name="retention_latent_forward",
    )(proj, proj, proj, proj, ob, s0_f, tabs["amat"], tabs["xif"], tabs["zf"], tabs["decf"])


def _attn_ctx_kernel(sink_ref, q_ref, k_ref, v_ref, o_ref):
    scale = HEAD_DIM ** -0.5
    outs = []
    for g in range(ATTN_KV):
        k_g = k_ref[:, g * HEAD_DIM:(g + 1) * HEAD_DIM].astype(BF16)
        v_g = v_ref[:, g * HEAD_DIM:(g + 1) * HEAD_DIM].astype(BF16)
        for r in range(ATTN_REP):
            hh = g * ATTN_REP + r
            q_h = q_ref[:, hh * HEAD_DIM:(hh + 1) * HEAD_DIM].astype(BF16)
            s = _dot_nt(q_h, k_g) * scale
            snk = sink_ref[hh]
            m = jnp.maximum(jnp.max(s, axis=-1, keepdims=True), snk)
            p = jnp.exp(s - m)
            den = jnp.sum(p, axis=-1, keepdims=True) + jnp.exp(snk - m)
            outs.append(_dot(p.astype(BF16), v_g) / den)
    o_ref[...] = jnp.concatenate(outs, axis=1).astype(o_ref.dtype)


def _attn_ctx(qkv, sink, batch, length):
    qw = ATTN_HEADS * HEAD_DIM
    kw = ATTN_KV * HEAD_DIM
    return pl.pallas_call(
        _attn_ctx_kernel,
        grid=(batch,),
        in_specs=[pl.BlockSpec(memory_space=pltpu.SMEM),
                  pl.BlockSpec((length, qw), lambda b: (b, 0)),
                  pl.BlockSpec((length, kw), lambda b: (b, qw // kw)),
                  pl.BlockSpec((length, kw), lambda b: (b, qw // kw + 1))],
        out_specs=pl.BlockSpec((length, qw), lambda b: (b, 0)),
        out_shape=jax.ShapeDtypeStruct((batch * length, qw), BF16),
        compiler_params=_cparams(1),
        name="attention_context",
    )(sink, qkv, qkv, qkv)


def _attn_lat_kernel(sink_ref, q_ref, k_ref, v_ref, kc_ref, vc_ref, o_ref, *, length):
    scale = HEAD_DIM ** -0.5
    band = 3 * QBLK
    j = pl.program_id(1)
    start = pl.multiple_of(jnp.clip((j - 1) * QBLK, 0, length - band), QBLK)
    rows = ATTN_REP * QBLK
    qpos = j * QBLK + lax.broadcasted_iota(I32, (rows, band), 0) % QBLK
    kpos = start + lax.broadcasted_iota(I32, (rows, band), 1)
    valid = jnp.abs(qpos - kpos) <= WINDOW
    rep_of_row = lax.broadcasted_iota(I32, (rows, 1), 0) // QBLK
    kb = k_ref[pl.ds(start, band), :]
    vb = v_ref[pl.ds(start, band), :]
    outs = []
    for g in range(ATTN_KV):
        cs = slice(g * HEAD_DIM, (g + 1) * HEAD_DIM)
        k_g = kb[:, cs].astype(BF16)
        v_g = vb[:, cs].astype(BF16)
        kc_g = kc_ref[:, cs].astype(BF16)
        vc_g = vc_ref[:, cs].astype(BF16)
        q_g = jnp.concatenate(
            [q_ref[:, (g * ATTN_REP + r) * HEAD_DIM:(g * ATTN_REP + r + 1) * HEAD_DIM] for r in range(ATTN_REP)],
            axis=0).astype(BF16)
        snk = jnp.zeros((rows, 1), F32)
        for r in range(ATTN_REP):
            snk = jnp.where(rep_of_row == r, sink_ref[g * ATTN_REP + r], snk)
        s_loc = jnp.where(valid, _dot_nt(q_g, k_g) * scale, NEG_INF)
        s_ctx = _dot_nt(q_g, kc_g) * scale
        m = jnp.maximum(jnp.maximum(jnp.max(s_loc, axis=-1, keepdims=True),
                                    jnp.max(s_ctx, axis=-1, keepdims=True)), snk)
        p_loc = jnp.exp(s_loc - m)
        p_ctx = jnp.exp(s_ctx - m)
        den = (jnp.sum(p_loc, axis=-1, keepdims=True) + jnp.sum(p_ctx, axis=-1, keepdims=True)) + jnp.exp(snk - m)
        o_g = (_dot(p_loc.astype(BF16), v_g) + _dot(p_ctx.astype(BF16), vc_g)) / den
        for r in range(ATTN_REP):
            outs.append(o_g[r * QBLK:(r + 1) * QBLK, :])
    o_ref[...] = jnp.concatenate(outs, axis=1).astype(o_ref.dtype)


def _attn_lat(qkv, sink, kc, vc, batch, length):
    qw = ATTN_HEADS * HEAD_DIM
    kw = ATTN_KV * HEAD_DIM
    nq = length // QBLK
    past = kc.shape[1]
    return pl.pallas_call(
        functools.partial(_attn_lat_kernel, length=length),
        grid=(batch, nq),
        in_specs=[pl.BlockSpec(memory_space=pltpu.SMEM),
                  pl.BlockSpec((QBLK, qw), lambda b, j: (b * nq + j, 0)),
                  pl.BlockSpec((length, kw), lambda b, j: (b, qw // kw)),
                  pl.BlockSpec((length, kw), lambda b, j: (b, qw // kw + 1)),
                  pl.BlockSpec((None, past, kw), lambda b, j: (b, 0, 0)),
                  pl.BlockSpec((None, past, kw), lambda b, j: (b, 0, 0))],
        out_specs=pl.BlockSpec((QBLK, qw), lambda b, j: (b * nq + j, 0)),
        out_shape=jax.ShapeDtypeStruct((batch * length, qw), BF16),
        compiler_params=_cparams(2),
        name="attention_latent",
    )(sink, qkv, qkv, qkv, kc, vc)


def _rope_tables(length):
    axis_dim = HEAD_DIM // 2
    pos = jnp.arange(length)
    row = (pos // GRID_W).astype(F32)
    colp = (pos % GRID_W).astype(F32)
    inv = ROPE_BASE ** (-jnp.arange(0, axis_dim, 2, dtype=F32) / axis_dim)
    ar = row[:, None] * inv[None, :]
    ac = colp[:, None] * inv[None, :]
    cos = jnp.concatenate([jnp.cos(ar), jnp.cos(ar), jnp.cos(ac), jnp.cos(ac)], axis=1)
    sin = jnp.concatenate([-jnp.sin(ar), jnp.sin(ar), -jnp.sin(ac), jnp.sin(ac)], axis=1)
    return jnp.concatenate([cos, cos], axis=1), jnp.concatenate([sin, sin], axis=1)


def _router_kernel(x_ref, g_ref, sh_ref, sc_ref, wr_ref, br_ref, h_ref, e_ref, p1_ref, p2_ref, n_ref, cnt_ref,
                   *, tiles_per_chunk):
    h = _norm_mod(x_ref[...], g_ref[...], sh_ref[...], sc_ref[...])
    _to_slabs(h_ref, h)
    logits = _dot3(h, wr_ref[...]) + br_ref[...]
    lane = lax.broadcasted_iota(I32, logits.shape, 1).astype(F32)
    ninf = -jnp.inf
    lg = jnp.where(lane < N_GROUPS, logits, ninf)
    gmax = jnp.max(lg, axis=-1, keepdims=True)
    gtop = jnp.min(jnp.where(lg == gmax, lane, float(LANE)), axis=-1, keepdims=True)
    pg_sel = 1.0 / jnp.sum(jnp.exp(lg - gmax), axis=-1, keepdims=True)
    lo = N_GROUPS + EXP_PER_GROUP * gtop
    le = jnp.where((lane >= lo) & (lane < lo + EXP_PER_GROUP), logits, ninf)
    v1 = jnp.max(le, axis=-1, keepdims=True)
    i1 = jnp.min(jnp.where(le == v1, lane, float(LANE)), axis=-1, keepdims=True)
    le2 = jnp.where(lane == i1, ninf, le)
    v2 = jnp.max(le2, axis=-1, keepdims=True)
    i2 = jnp.min(jnp.where(le2 == v2, lane, float(LANE)), axis=-1, keepdims=True)
    t = jnp.exp(v2 - v1)
    gate1 = pg_sel * (1.0 / (1.0 + t))
    gate2 = pg_sel * (t / (1.0 + t))
    p1_ref[...] = jnp.broadcast_to(gate1, p1_ref.shape)
    p2_ref[...] = jnp.broadcast_to(gate2, p2_ref.shape)

    e1 = i1 - N_GROUPS
    e2 = i2 - N_GROUPS
    tm = logits.shape[0]

    @pl.when(pl.program_id(0) % tiles_per_chunk == 0)
    def _():
        cnt_ref[...] = jnp.zeros_like(cnt_ref)

    carry = cnt_ref[0:1, :]
    oh1 = (lane == e1).astype(F32)
    oh2 = (lane == e2).astype(F32)
    tri = (lax.broadcasted_iota(I32, (tm, tm), 0) >= lax.broadcasted_iota(I32, (tm, tm), 1)).astype(BF16)
    cs1 = _dot(tri, oh1.astype(BF16))
    cs2 = _dot(tri, oh2.astype(BF16))
    tot1 = cs1[tm - 1:tm, :]
    tot2 = cs2[tm - 1:tm, :]
    rank1 = jnp.sum(oh1 * (carry + cs1 - 1.0), axis=-1, keepdims=True)
    rank2 = jnp.sum(oh2 * ((carry + tot1) + cs2 - 1.0), axis=-1, keepdims=True)
    code1 = rank1 * float(N_EXP) + e1
    code2 = rank2 * float(N_EXP) + e2
    e_ref[...] = jnp.where(lane == 0, code1, jnp.where(lane == 1, code2, 0.0)).astype(I32)
    new_cnt = (carry + tot1) + tot2
    cnt_ref[...] = jnp.broadcast_to(new_cnt, cnt_ref.shape)
    n_ref[...] = jnp.broadcast_to(new_cnt, n_ref.shape).astype(I32)


def _router(x, mods, layer, norm_g, wr, br, is_lat):
    t = x.shape[0]
    tm = 256
    seg_fn = _seg_of(is_lat, tm)
    tiles_per_chunk = SEG // tm
    return pl.pallas_call(
        functools.partial(_router_kernel, tiles_per_chunk=tiles_per_chunk),
        grid=(t // tm,),
        in_specs=[pl.BlockSpec((tm, D), lambda i: (i, 0)),
                  pl.BlockSpec((1, D), lambda i: (0, 0)),
                  _mod_spec(layer, 3, seg_fn),
                  _mod_spec(layer, 4, seg_fn),
                  pl.BlockSpec((D, LANE), lambda i: (0, 0)),
                  pl.BlockSpec((1, LANE), lambda i: (0, 0))],
        out_specs=[pl.BlockSpec((tm * SUB, LANE), lambda i: (i, 0)),
                   pl.BlockSpec((tm, LANE), lambda i: (i, 0)),
                   pl.BlockSpec((tm, LANE), lambda i: (i, 0)),
                   pl.BlockSpec((tm, LANE), lambda i: (i, 0)),
                   pl.BlockSpec((SUB, LANE), lambda i: (i // tiles_per_chunk, 0))],
        out_shape=[jax.ShapeDtypeStruct((t * SUB, LANE), F32),
                   jax.ShapeDtypeStruct((t, LANE), I32),
                   jax.ShapeDtypeStruct((t, LANE), F32),
                   jax.ShapeDtypeStruct((t, LANE), F32),
                   jax.ShapeDtypeStruct((t // SEG * SUB, LANE), I32)],
        scratch_shapes=[pltpu.VMEM((SUB, LANE), F32)],
        compiler_params=_cparams(1),
        name="moe_router",
    )(x, norm_g.reshape(1, D), mods, mods, wr, br)


def _moe_kernel(dest_ref, meta_ref, h_hbm, g1_hbm, g2_hbm, w1_ref, w3_ref, w2_ref, y_hbm,
                h_vm, y_vm, g_vm, xt, ot, gt, w1b, w3b, w2b, slot_tab, sem, *, tc, a_pad):
    chunk = pl.program_id(0)
    e = pl.program_id(1)
    stride = MOE_TILE_STRIDE
    n_lane_chunks = D // LANE
    h_rows = tc * SUB
    n_assign = 2 * tc
    slab_mask = (1 << 16) - 1

    def h_copy():
        return pltpu.make_async_copy(h_hbm.at[pl.ds(chunk * h_rows, h_rows)], h_vm.at[pl.ds(0, h_rows)], sem.at[0])

    def y_copy():
        return pltpu.make_async_copy(y_vm.at[pl.ds(0, h_rows)], y_hbm.at[pl.ds(chunk * h_rows, h_rows)], sem.at[1])

    def g_copy(k, g_hbm):
        return pltpu.make_async_copy(g_hbm.at[pl.ds(chunk * tc, tc)], g_vm.at[pl.ds(k * tc, tc)], sem.at[2 + k])

    @pl.when(e == 0)
    def _():
        h_copy().start()
        g_copy(0, g1_hbm).start()
        g_copy(1, g2_hbm).start()

        def zero(i, carry):
            y_vm[pl.ds(pl.multiple_of(i * 512, 512), 512), :] = jnp.zeros((512, LANE), F32)
            return carry

        lax.fori_loop(0, h_rows // 512, zero, 0)
        y_vm[pl.ds(h_rows, SUB), :] = jnp.zeros((SUB, LANE), F32)
        h_vm[pl.ds(h_rows, SUB), :] = jnp.zeros((SUB, LANE), F32)

        meta_base = chunk * 3 * N_EXP

        def fill_run(ex, carry):
            run_start = meta_ref[meta_base + ex]

            def fill(s, c2):
                slot_tab[run_start + s] = h_rows
                return c2

            return lax.fori_loop(meta_ref[meta_base + 2 * N_EXP + ex], meta_ref[meta_base + N_EXP + ex], fill, carry)

        lax.fori_loop(0, N_EXP, fill_run, 0)

        tok_step = SUB + (1 << 16)

        def invert(i, carry):
            entry0 = i * ((SUB // 2) * tok_step)
            code0 = chunk * n_assign + i * SUB
            for u in range(SUB):
                code = dest_ref[code0 + u]
                slot = meta_ref[meta_base + (code & (N_EXP - 1))] + lax.shift_right_logical(code, 5)
                slot_tab[slot] = entry0 + ((u // 2) * tok_step + (u % 2) * (tc << 16))
            return carry

        lax.fori_loop(0, n_assign // SUB, invert, 0)
        h_copy().wait()
        g_copy(0, g1_hbm).wait()
        g_copy(1, g2_hbm).wait()

    w1b[...] = w1_ref[...].astype(BF16)
    w3b[...] = w3_ref[...].astype(BF16)
    w2b[...] = w2_ref[...].astype(BF16)

    start = meta_ref[chunk * 3 * N_EXP + e]
    n_pad = meta_ref[chunk * 3 * N_EXP + N_EXP + e]

    def process(off, m):
        off = pl.multiple_of(off, MOE_PAD)

        def gather(i, carry):
            for u in range(SUB):
                r = i * SUB + u
                entry = slot_tab[off + r]
                t8 = pl.multiple_of(entry & slab_mask, SUB)
                xt[pl.ds(r, SUB, stride=stride), :] = h_vm[pl.ds(t8, SUB), :]
                gt[pl.ds(r, 1), :] = g_vm[pl.ds(lax.shift_right_logical(entry, 16), 1), :]
            return carry

        lax.fori_loop(0, m // SUB, gather, 0)
        x = jnp.concatenate([xt[pl.ds(c * stride, m), :] for c in range(n_lane_chunks)], axis=1).astype(BF16)
        a = _dot(x, w1b[...])
        b = _dot(x, w3b[...])
        hm = (_silu(a) * b).astype(BF16)
        out = _dot(hm, w2b[...])
        gate = gt[pl.ds(0, m), :]
        for c in range(n_lane_chunks):
            ot[pl.ds(c * stride, m), :] = out[:, c * LANE:(c + 1) * LANE] * gate

        def scatter(i, carry):
            for half in range(2):
                t8s, news = [], []
                for u in range(4):
                    r = i * SUB + half * 4 + u
                    t8 = pl.multiple_of(slot_tab[off + r] & slab_mask, SUB)
                    t8s.append(t8)
                    news.append(y_vm[pl.ds(t8, SUB), :] + ot[pl.ds(r, SUB, stride=stride), :])
                for u in range(4):
                    y_vm[pl.ds(t8s[u], SUB), :] = news[u]
            return carry

        lax.fori_loop(0, m // SUB, scatter, 0)

    n_full = n_pad // 256

    def full_body(i, carry):
        process(start + i * 256, 256)
        return carry

    lax.fori_loop(0, n_full, full_body, 0)
    rem = start + n_full * 256

    @pl.when((n_pad & 128) != 0)
    def _():
        process(rem, 128)

    @pl.when((n_pad & 64) != 0)
    def _():
        process(rem + (n_pad & 128), 64)

    @pl.when(e == N_EXP - 1)
    def _():
        y_copy().start()
        y_copy().wait()


def _moe(h_slabs, codes, counts, gate1, gate2, w1, w3, w2, layer):
    t = codes.shape[0]
    tc = SEG
    n_chunks = t // tc
    a_pad = 2 * tc + N_EXP * MOE_PAD
    padded = (counts + MOE_PAD - 1) // MOE_PAD * MOE_PAD
    meta = jnp.concatenate([jnp.cumsum(padded, axis=1) - padded, padded, counts], axis=1).astype(I32)
    dest = codes
    slab_rows = (tc + 1) * SUB
    tile_rows = (D // LANE) * MOE_TILE_STRIDE
    hbm_spec = pl.BlockSpec(memory_space=pl.ANY)
    grid_spec = pltpu.PrefetchScalarGridSpec(
        num_scalar_prefetch=2,
        grid=(n_chunks, N_EXP),
        in_specs=[hbm_spec, hbm_spec, hbm_spec,
                  pl.BlockSpec((None, None, D, D_FF), lambda c, e, *_: (layer, e, 0, 0)),
                  pl.BlockSpec((None, None, D, D_FF), lambda c, e, *_: (layer, e, 0, 0)),
                  pl.BlockSpec((None, None, D_FF, D), lambda c, e, *_: (layer, e, 0, 0))],
        out_specs=pl.BlockSpec(memory_space=pl.ANY),
        scratch_shapes=[pltpu.VMEM((slab_rows, LANE), F32),
                        pltpu.VMEM((slab_rows, LANE), F32),
                        pltpu.VMEM((2 * tc, LANE), F32),
                        pltpu.VMEM((tile_rows, LANE), F32),
                        pltpu.VMEM((tile_rows, LANE), F32),
                        pltpu.VMEM((256, LANE), F32),
                        pltpu.VMEM((D, D_FF), BF16),
                        pltpu.VMEM((D, D_FF), BF16),
                        pltpu.VMEM((D_FF, D), BF16),
                        pltpu.SMEM((a_pad,), I32),
                        pltpu.SemaphoreType.DMA((4,))],
    )
    return pl.pallas_call(
        functools.partial(_moe_kernel, tc=tc, a_pad=a_pad),
        grid_spec=grid_spec,
        out_shape=jax.ShapeDtypeStruct((t * SUB, LANE), F32),
        compiler_params=pltpu.CompilerParams(dimension_semantics=("arbitrary", "arbitrary"),
                                             vmem_limit_bytes=V7X_VMEM_LIMIT_MOE),
        name="moe_experts",
    )(dest.reshape(n_chunks * 2 * tc), meta.reshape(n_chunks * 3 * N_EXP), h_slabs, gate1, gate2, w1, w3, w2)


def kernel(x_prompt, x_sample, c, cache_attn_k, cache_attn_v, state_ret_fwd, state_ret_bwd, c_ctx, adaln_w, adaln_b, norm1_g, norm2_g, final_g, fnet_w, ret_w_in, ret_w_out, ret_decay_fwd, ret_decay_bwd, attn_w_in, attn_w_out, attn_sink, router_group_w, router_group_b, router_expert_w, router_expert_b, moe_w1, moe_w3, moe_w2):
    bp, lp, _ = x_prompt.shape
    bs, ls, _ = x_sample.shape
    xs = {False: x_prompt.reshape(bp * lp, D), True: x_sample.reshape(bs * ls, D)}
    dims = {False: (bp, lp), True: (bs, ls)}

    cond8 = jnp.concatenate([c_ctx[None, :], c, jnp.zeros((SUB - 1 - bs, D), F32)], axis=0)
    mods = _adaln(cond8, adaln_w, adaln_b).reshape(DEPTH, SUB, 6, 1, D)

    rope_lat = _rope_tables(ls)
    rope = {False: (jnp.ones((bp * lp, LANE), F32), jnp.zeros((bp * lp, LANE), F32)),
            True: (jnp.tile(rope_lat[0], (bs, 1)), jnp.tile(rope_lat[1], (bs, 1)))}

    extras = {}
    for i in range(DEPTH):
        kind, j = i % 3, i // 3
        if kind == 0:
            lw = dict(w_fold=_fnet_fold(fnet_w[j]))
        elif kind == 1:
            lw = dict(w_in=ret_w_in[j].astype(BF16), w_out=ret_w_out[j].astype(BF16),
                      tabs=_ret_tables(ret_decay_fwd[j], ret_decay_bwd[j]),
                      s0_f=state_ret_fwd[:, j], s0_b=state_ret_bwd[:, j])
        else:
            kvw = ATTN_KV * HEAD_DIM
            lw = dict(w_in=attn_w_in[j].astype(BF16), w_out=attn_w_out[j].astype(BF16), sink=attn_sink[j],
                      kc=cache_attn_k[:, j].reshape(bs, -1, kvw), vc=cache_attn_v[:, j].reshape(bs, -1, kvw))
        wr = jnp.concatenate([router_group_w[i], jnp.transpose(router_expert_w[i], (1, 0, 2)).reshape(D, N_EXP),
                              jnp.zeros((D, LANE - N_GROUPS - N_EXP), F32)], axis=1)
        br = jnp.concatenate([router_group_b[i], router_expert_b[i].reshape(N_EXP),
                              jnp.zeros((LANE - N_GROUPS - N_EXP,), F32)]).reshape(1, LANE)

        for is_lat in (False, True):
            b_, l_ = dims[is_lat]
            x = _mixer(i, xs[is_lat], mods, norm1_g[i], lw, is_lat, b_, l_, rope[is_lat], extras)
            xs[is_lat] = _ffn(i, x, mods, norm2_g[i], wr, br, moe_w1, moe_w3, moe_w2, is_lat)

    y_prompt = _final_norm(xs[False], final_g).reshape(bp, lp, D)
    y_sample = _final_norm(xs[True], final_g).reshape(bs, ls, D)
    return (y_prompt, y_sample, extras["k"], extras["v"], extras["sf"], extras["sb"])


def _mixer(i, x, mods, norm_g, lw, is_lat, b_, l_, rope, extras):
    kind = i % 3
    if kind == 0:
        y = _proj(x, mods, i, 0, norm_g, lw["w_fold"], is_lat, BF16)
        return _seqmix(y, x, mods, i, is_lat, b_, l_)
    if kind == 1:
        proj = _proj(x, mods, i, 0, norm_g, lw["w_in"], is_lat, BF16)
        if is_lat:
            yr = _ret_lat(proj, lw["tabs"], lw["s0_f"], lw["s0_b"], b_, l_)
        else:
            yr, sf, sb = _ret_ctx(proj, lw["tabs"], b_)
            extras["sf"], extras["sb"] = sf[:, None], sb[:, None]
        return _linres(yr, lw["w_out"], x, mods, i, 2, is_lat)
    qkv = _proj(x, mods, i, 0, norm_g, lw["w_in"], is_lat, F32, rope=rope)
    if is_lat:
        o = _attn_lat(qkv, lw["sink"], lw["kc"], lw["vc"], b_, l_)
    else:
        qw = ATTN_HEADS * HEAD_DIM
        kw = ATTN_KV * HEAD_DIM
        extras["k"] = qkv[:, qw:qw + kw].reshape(b_, 1, l_, ATTN_KV, HEAD_DIM)
        extras["v"] = qkv[:, qw + kw:].reshape(b_, 1, l_, ATTN_KV, HEAD_DIM)
        o = _attn_ctx(qkv, lw["sink"], b_, l_)
    return _linres(o, lw["w_out"], x, mods, i, 2, is_lat)


def _ffn(i, x, mods, norm_g, wr, br, w1, w3, w2, is_lat):
    h_slabs, codes, gate1, gate2, counts = _router(x, mods, i, norm_g, wr, br, is_lat)
    counts = counts.reshape(-1, SUB, LANE)[:, 0, :N_EXP]
    y_slabs = _moe(h_slabs, codes[:, :2], counts, gate1, gate2, w1, w3, w2, i)
    return _residual(x, y_slabs, mods, i, 5, is_lat)
```

```python
import functools
import math

import jax
import jax.numpy as jnp
from jax import lax
from jax.experimental import pallas as pl
from jax.experimental.pallas import tpu as pltpu

F32 = jnp.float32
BF16 = jnp.bfloat16
I32 = jnp.int32

D = 1024
DEPTH = 4
SEG = 4096
FNET_GROUP_DIM = 256
RET_HEADS, RET_DK, RET_DV = 4, 256, 512
RET_C = 256
ATTN_HEADS, ATTN_KV, HEAD_DIM, ATTN_REP = 16, 4, 64, 4
WINDOW, QBLK = 128, 128
GRID_W = 64
ROPE_BASE = 10000.0
N_EXP, EXP_PER_GROUP, N_GROUPS, D_FF = 32, 8, 4, 512
EPS = 1e-6
NEG_INF = -1e30
LANE = 128
SUB = 8

V7X_VMEM_LIMIT = 56 * 1024 * 1024
V7X_VMEM_LIMIT_MOE = 59 * 1024 * 1024
MOE_PAD = 64
MOE_TILE_STRIDE = 260


def _cparams(n_axes):
    return pltpu.CompilerParams(dimension_semantics=("arbitrary",) * n_axes, vmem_limit_bytes=V7X_VMEM_LIMIT)


def _dot(a, b):
    return jnp.dot(a, b, preferred_element_type=F32)


def _dot_nt(a, b):
    return lax.dot_general(a, b, (((1,), (1,)), ((), ())), preferred_element_type=F32)


def _split(x):
    hi = x.astype(BF16)
    lo = (x - hi.astype(F32)).astype(BF16)
    return hi, lo


def _dot3(a, b):
    ah, al = _split(a)
    bh, bl = _split(b)
    return _dot(ah, bh) + (_dot(ah, bl) + _dot(al, bh))


def _silu(x):
    return x * jax.nn.sigmoid(x)


def _norm_mod(x, g, sh, sc):
    ms = jnp.mean(x * x, axis=-1, keepdims=True)
    return (x * lax.rsqrt(ms + EPS) * g) * (1.0 + sc) + sh


def _seg_of(is_lat, tm):
    if is_lat:
        return lambda i: 1 + (i * tm) // SEG
    return lambda i: 0


def _mod_spec(layer, kind, seg_fn):
    return pl.BlockSpec((None, None, None, 1, D), lambda i, *_: (layer, seg_fn(i), kind, 0, 0))


def _adaln_kernel(c_ref, w_ref, b_ref, o_ref):
    o_ref[...] = _dot3(_silu(c_ref[...]), w_ref[...]) + b_ref[...]


def _adaln(cond8, w, b):
    depth, _, n = w.shape
    tn = 1536
    return pl.pallas_call(
        _adaln_kernel,
        grid=(depth, n // tn),
        in_specs=[pl.BlockSpec((SUB, D), lambda l, j: (0, 0)),
                  pl.BlockSpec((None, D, tn), lambda l, j: (l, 0, j)),
                  pl.BlockSpec((None, 1, tn), lambda l, j: (l, 0, j))],
        out_specs=pl.BlockSpec((None, SUB, tn), lambda l, j: (l, 0, j)),
        out_shape=jax.ShapeDtypeStruct((depth, SUB, n), F32),
        compiler_params=_cparams(2),
        name="adaln",
    )(cond8, w, b.reshape(depth, 1, n))


def _rope(x, cos_t, sin_t):
    w = x.shape[1]
    reps = w // LANE
    c = jnp.concatenate([cos_t] * reps, axis=1)
    s = jnp.concatenate([sin_t] * reps, axis=1)
    lane = lax.broadcasted_iota(I32, x.shape, 1)
    first = (lane % 32) < 16
    partner = jnp.where(first, pltpu.roll(x, w - 16, 1), pltpu.roll(x, 16, 1))
    return x * c + partner * s


def _proj_kernel(x_ref, g_ref, sh_ref, sc_ref, w_ref, *rest, n_rope):
    if n_rope:
        cos_ref, sin_ref, o_ref = rest
    else:
        (o_ref,) = rest
    hb = _norm_mod(x_ref[...], g_ref[...], sh_ref[...], sc_ref[...]).astype(BF16)
    n = w_ref.shape[1]
    tn = 512
    if n_rope:
        q_w = ATTN_HEADS * HEAD_DIM
        k_w = ATTN_KV * HEAD_DIM
        q = _dot(hb, w_ref[:, :q_w])
        o_ref[:, :q_w] = _rope(q, cos_ref[...], sin_ref[...])
        k = _dot(hb, w_ref[:, q_w:q_w + k_w])
        o_ref[:, q_w:q_w + k_w] = _rope(k, cos_ref[...], sin_ref[...])
        o_ref[:, q_w + k_w:] = _dot(hb, w_ref[:, q_w + k_w:])
    else:
        for j in range(n // tn):
            o_ref[:, j * tn:(j + 1) * tn] = _dot(hb, w_ref[:, j * tn:(j + 1) * tn]).astype(o_ref.dtype)


def _proj(x, mods, layer, sub, norm_g, w_bf16, is_lat, out_dtype, rope=None):
    t = x.shape[0]
    n = w_bf16.shape[1]
    tm = 256
    seg_fn = _seg_of(is_lat, tm)
    in_specs = [pl.BlockSpec((tm, D), lambda i: (i, 0)),
                pl.BlockSpec((1, D), lambda i: (0, 0)),
                _mod_spec(layer, 3 * sub + 0, seg_fn),
                _mod_spec(layer, 3 * sub + 1, seg_fn),
                pl.BlockSpec((D, n), lambda i: (0, 0))]
    args = [x, norm_g.reshape(1, D), mods, mods, w_bf16]
    if rope is not None:
        in_specs += [pl.BlockSpec((tm, LANE), lambda i: (i, 0))] * 2
        args += list(rope)
    return pl.pallas_call(
        functools.partial(_proj_kernel, n_rope=rope is not None),
        grid=(t // tm,),
        in_specs=in_specs,
        out_specs=pl.BlockSpec((tm, n), lambda i: (i, 0)),
        out_shape=jax.ShapeDtypeStruct((t, n), out_dtype),
        compiler_params=_cparams(1),
        name="norm_mod_proj",
    )(*args)


def _linres_kernel(y_ref, w_ref, x_ref, g_ref, o_ref):
    o_ref[...] = x_ref[...] + g_ref[...] * _dot(y_ref[...].astype(BF16), w_ref[...])


def _linres(y, w_bf16, x, mods, layer, kind, is_lat):
    t, k = y.shape
    tm = 256
    seg_fn = _seg_of(is_lat, tm)
    return pl.pallas_call(
        _linres_kernel,
        grid=(t // tm,),
        in_specs=[pl.BlockSpec((tm, k), lambda i: (i, 0)),
                  pl.BlockSpec((k, D), lambda i: (0, 0)),
                  pl.BlockSpec((tm, D), lambda i: (i, 0)),
                  _mod_spec(layer, kind, seg_fn)],
        out_specs=pl.BlockSpec((tm, D), lambda i: (i, 0)),
        out_shape=jax.ShapeDtypeStruct((t, D), F32),
        compiler_params=_cparams(1),
        name="linear_residual",
    )(y, w_bf16, x, mods)


def _to_slabs(slab_ref, x):
    tm = x.shape[0]
    for c in range(D // LANE):
        slab_ref[pl.ds(c, tm, stride=SUB), :] = x[:, c * LANE:(c + 1) * LANE]


def _from_slabs(slab_ref, tm):
    return jnp.concatenate([slab_ref[pl.ds(c, tm, stride=SUB), :] for c in range(D // LANE)], axis=1)


def _residual_kernel(x_ref, y_ref, g_ref, o_ref):
    o_ref[...] = x_ref[...] + g_ref[...] * _from_slabs(y_ref, x_ref.shape[0])


def _residual(x, y_slabs, mods, layer, kind, is_lat):
    t = x.shape[0]
    tm = 512
    seg_fn = _seg_of(is_lat, tm)
    return pl.pallas_call(
        _residual_kernel,
        grid=(t // tm,),
        in_specs=[pl.BlockSpec((tm, D), lambda i: (i, 0)),
                  pl.BlockSpec((tm * SUB, LANE), lambda i: (i, 0)),
                  _mod_spec(layer, kind, seg_fn)],
        out_specs=pl.BlockSpec((tm, D), lambda i: (i, 0)),
        out_shape=jax.ShapeDtypeStruct((t, D), F32),
        compiler_params=_cparams(1),
        name="gated_residual",
    )(x, y_slabs, mods)


def _final_norm_kernel(x_ref, g_ref, o_ref):
    x = x_ref[...]
    ms = jnp.mean(x * x, axis=-1, keepdims=True)
    o_ref[...] = x * lax.rsqrt(ms + EPS) * g_ref[...]


def _final_norm(x, g):
    t = x.shape[0]
    tm = 512
    return pl.pallas_call(
        _final_norm_kernel,
        grid=(t // tm,),
        in_specs=[pl.BlockSpec((tm, D), lambda i: (i, 0)), pl.BlockSpec((1, D), lambda i: (0, 0))],
        out_specs=pl.BlockSpec((tm, D), lambda i: (i, 0)),
        out_shape=jax.ShapeDtypeStruct((t, D), F32),
        compiler_params=_cparams(1),
        name="final_rmsnorm",
    )(x, g.reshape(1, D))


def _fold_kernel(c_ref, s_ref, w_ref, o_ref):
    w = w_ref[...]
    o_ref[:, :D] = _dot3(c_ref[...], w).astype(BF16)
    o_ref[:, D:] = _dot3(s_ref[...], w).astype(BF16)


def _fnet_fold(w):
    n = jnp.arange(FNET_GROUP_DIM, dtype=I32)
    ang = ((n[:, None] * n[None, :]) % FNET_GROUP_DIM).astype(F32) * (2.0 * math.pi / FNET_GROUP_DIM)
    sc = 1.0 / math.sqrt(FNET_GROUP_DIM)
    gd = FNET_GROUP_DIM
    return pl.pallas_call(
        _fold_kernel,
        grid=(D // gd,),
        in_specs=[pl.BlockSpec((gd, gd), lambda g: (0, 0)),
                  pl.BlockSpec((gd, gd), lambda g: (0, 0)),
                  pl.BlockSpec((gd, D), lambda g: (g, 0))],
        out_specs=pl.BlockSpec((gd, 2 * D), lambda g: (g, 0)),
        out_shape=jax.ShapeDtypeStruct((D, 2 * D), BF16),
        compiler_params=_cparams(1),
        name="fnet_fold_channel_dft",
    )(jnp.cos(ang) * sc, jnp.sin(ang) * sc, w)


def _seq_dft(length):
    n = jnp.arange(length, dtype=I32)
    sc = 1.0 / math.sqrt(length)
    split = 64
    if length <= split * 4:
        ang = ((n[:, None] * n[None, :]) % length).astype(F32) * (2.0 * math.pi / length)
        return (jnp.cos(ang) * sc).astype(BF16), (jnp.sin(ang) * (-sc)).astype(BF16)
    hi = jnp.arange(length // split, dtype=I32)
    lo = jnp.arange(split, dtype=I32)
    ang_a = ((hi[:, None] * n[None, :]) % (length // split)).astype(F32) * (2.0 * math.pi * split / length)
    ang_b = ((lo[:, None] * n[None, :]) % length).astype(F32) * (2.0 * math.pi / length)
    ca, sa = jnp.cos(ang_a)[:, None, :], jnp.sin(ang_a)[:, None, :]
    cb, sb = (jnp.cos(ang_b) * sc)[None, :, :], (jnp.sin(ang_b) * sc)[None, :, :]
    cm = (ca * cb - sa * sb).reshape(length, length)
    sm = (sa * cb + ca * sb).reshape(length, length)
    return cm.astype(BF16), (-sm).astype(BF16)


def _seqmix_kernel(c_ref, s_ref, y_ref, x_ref, g_ref, o_ref, acc_ref, *, nk):
    k = pl.program_id(2)

    @pl.when(k == 0)
    def _():
        acc_ref[...] = jnp.zeros_like(acc_ref)

    acc_ref[...] += _dot(c_ref[...], y_ref[:, :D]) + _dot(s_ref[...], y_ref[:, D:])

    @pl.when(k == nk - 1)
    def _():
        o_ref[...] = x_ref[...] + g_ref[...] * acc_ref[...]


def _seqmix(y, x, mods, layer, is_lat, batch, length):
    cm, sm = _seq_dft(length)
    tm = min(length, 1024)
    tk = min(length, 1024)
    nm, nk = length // tm, length // tk
    seg = (lambda b: 1 + b) if is_lat else (lambda b: 0)
    out = pl.pallas_call(
        functools.partial(_seqmix_kernel, nk=nk),
        grid=(batch, nm, nk),
        in_specs=[pl.BlockSpec((tm, tk), lambda b, m, k: (m, k)),
                  pl.BlockSpec((tm, tk), lambda b, m, k: (m, k)),
                  pl.BlockSpec((None, tk, 2 * D), lambda b, m, k: (b, k, 0)),
                  pl.BlockSpec((None, tm, D), lambda b, m, k: (b, m, 0)),
                  pl.BlockSpec((None, None, None, 1, D), lambda b, m, k: (layer, seg(b), 2, 0, 0))],
        out_specs=pl.BlockSpec((None, tm, D), lambda b, m, k: (b, m, 0)),
        out_shape=jax.ShapeDtypeStruct((batch, length, D), F32),
        scratch_shapes=[pltpu.VMEM((tm, D), F32)],
        compiler_params=_cparams(3),
        name="fnet_seq_dft",
    )(cm, sm, y.reshape(batch, length, 2 * D), x.reshape(batch, length, D), mods)
    return out.reshape(batch * length, D)


def _ret_tables(dec_f, dec_b):
    c = RET_C
    lgf = jax.nn.log_sigmoid(dec_f.astype(F32))[:, None, None]
    lgb = jax.nn.log_sigmoid(dec_b.astype(F32))[:, None, None]
    idx = jnp.arange(c, dtype=F32)
    diff = idx[:, None] - idx[None, :]
    kscale = RET_DK ** -0.5
    amat = jnp.where(diff >= 0, jnp.exp(jnp.maximum(diff, 0.0)[None] * lgf),
                     jnp.exp(jnp.maximum(-diff, 0.0)[None] * lgb)) * kscale
    col = idx[None, :, None]
    ones_k = jnp.ones((1, 1, RET_DK), F32)
    ones_v = jnp.ones((1, 1, RET_DV), F32)
    return dict(
        amat=amat,
        zf=jnp.exp((c - 1.0 - col) * lgf) * kscale * ones_k,
        zb=jnp.exp(col * lgb) * kscale * ones_k,
        xif=jnp.exp((col + 1.0) * lgf) * ones_v,
        xib=jnp.exp((c - col) * lgb) * ones_v,
        decf=jnp.exp(c * lgf) * ones_v,
        decb=jnp.exp(c * lgb) * ones_v,
    )


def _state_update(k_h, z_h, v_h):
    kz_t = jnp.transpose(k_h.astype(F32) * z_h).astype(BF16)
    return _dot(kz_t, v_h)


def _group_norm_gate(o, g_h):
    mu = jnp.mean(o, axis=-1, keepdims=True)
    d = o - mu
    var = jnp.mean(d * d, axis=-1, keepdims=True)
    g = g_h.astype(F32)
    return (_silu(g) * (d * lax.rsqrt(var + EPS))).astype(BF16)


def _ret_ctx_kernel(q_ref, k_ref, v_ref, g_ref, a_ref, zf_ref, zb_ref, y_ref, sf_ref, sb_ref):
    for h in range(RET_HEADS):
        q_h = q_ref[:, h * RET_DK:(h + 1) * RET_DK]
        k_h = k_ref[:, h * RET_DK:(h + 1) * RET_DK]
        v_h = v_ref[:, h * RET_DV:(h + 1) * RET_DV]
        inner = (_dot_nt(q_h, k_h) * a_ref[h]).astype(BF16)
        o = _dot(inner, v_h)
        y_ref[:, h * RET_DV:(h + 1) * RET_DV] = _group_norm_gate(o, g_ref[:, h * RET_DV:(h + 1) * RET_DV])
        sf_ref[h] = _state_update(k_h, zf_ref[h], v_h)
        sb_ref[h] = _state_update(k_h, zb_ref[h], v_h)


def _ret_ctx(proj, tabs, batch):
    c = RET_C
    qw, vw = RET_HEADS * RET_DK, RET_HEADS * RET_DV
    tab_spec = lambda arr: pl.BlockSpec(arr.shape, lambda b: (0, 0, 0))
    st_shape = jax.ShapeDtypeStruct((batch, RET_HEADS, RET_DK, RET_DV), F32)
    st_spec = pl.BlockSpec((None, RET_HEADS, RET_DK, RET_DV), lambda b: (b, 0, 0, 0))
    return pl.pallas_call(
        _ret_ctx_kernel,
        grid=(batch,),
        in_specs=[pl.BlockSpec((c, qw), lambda b: (b, 0)),
                  pl.BlockSpec((c, qw), lambda b: (b, 1)),
                  pl.BlockSpec((c, vw), lambda b: (b, 1)),
                  pl.BlockSpec((c, vw), lambda b: (b, 2)),
                  tab_spec(tabs["amat"]), tab_spec(tabs["zf"]), tab_spec(tabs["zb"])],
        out_specs=[pl.BlockSpec((c, vw), lambda b: (b, 0)), st_spec, st_spec],
        out_shape=[jax.ShapeDtypeStruct((batch * c, vw), BF16), st_shape, st_shape],
        compiler_params=_cparams(1),
        name="retention_context",
    )(proj, proj, proj, proj, tabs["amat"], tabs["zf"], tabs["zb"])


def _ret_bwd_kernel(q_ref, k_ref, v_ref, s0_ref, xib_ref, zb_ref, decb_ref, ob_ref, s_ref):
    @pl.when(pl.program_id(1) == 0)
    def _():
        s_ref[...] = s0_ref[...]

    for h in range(RET_HEADS):
        q_h = q_ref[:, h * RET_DK:(h + 1) * RET_DK]
        k_h = k_ref[:, h * RET_DK:(h + 1) * RET_DK]
        v_h = v_ref[:, h * RET_DV:(h + 1) * RET_DV]
        s = s_ref[h]
        ob_ref[:, h * RET_DV:(h + 1) * RET_DV] = _dot(q_h, s.astype(BF16)) * xib_ref[h]
        s_ref[h] = s * decb_ref[h] + _state_update(k_h, zb_ref[h], v_h)


def _ret_fwd_kernel(q_ref, k_ref, v_ref, g_ref, ob_ref, s0_ref, a_ref, xif_ref, zf_ref, decf_ref, y_ref, s_ref):
    @pl.when(pl.program_id(1) == 0)
    def _():
        s_ref[...] = s0_ref[...]

    for h in range(RET_HEADS):
        q_h = q_ref[:, h * RET_DK:(h + 1) * RET_DK]
        k_h = k_ref[:, h * RET_DK:(h + 1) * RET_DK]
        v_h = v_ref[:, h * RET_DV:(h + 1) * RET_DV]
        s = s_ref[h]
        inner = (_dot_nt(q_h, k_h) * a_ref[h]).astype(BF16)
        o = (_dot(inner, v_h) + _dot(q_h, s.astype(BF16)) * xif_ref[h]) + ob_ref[:, h * RET_DV:(h + 1) * RET_DV]
        y_ref[:, h * RET_DV:(h + 1) * RET_DV] = _group_norm_gate(o, g_ref[:, h * RET_DV:(h + 1) * RET_DV])
        s_ref[h] = s * decf_ref[h] + _state_update(k_h, zf_ref[h], v_h)


def _ret_lat(proj, tabs, s0_f, s0_b, batch, length):
    c = RET_C
    nc = length // c
    qw, vw = RET_HEADS * RET_DK, RET_HEADS * RET_DV
    t = batch * length
    tab_spec = lambda arr: pl.BlockSpec(arr.shape, lambda b, j: (0, 0, 0))
    st_spec = pl.BlockSpec((None, RET_HEADS, RET_DK, RET_DV), lambda b, j: (b, 0, 0, 0))
    st_scratch = pltpu.VMEM((RET_HEADS, RET_DK, RET_DV), F32)
    rev = lambda b, j: b * nc + (nc - 1 - j)
    fwd = lambda b, j: b * nc + j
    ob = pl.pallas_call(
        _ret_bwd_kernel,
        grid=(batch, nc),
        in_specs=[pl.BlockSpec((c, qw), lambda b, j: (rev(b, j), 0)),
                  pl.BlockSpec((c, qw), lambda b, j: (rev(b, j), 1)),
                  pl.BlockSpec((c, vw), lambda b, j: (rev(b, j), 1)),
                  st_spec, tab_spec(tabs["xib"]), tab_spec(tabs["zb"]), tab_spec(tabs["decb"])],
        out_specs=pl.BlockSpec((c, vw), lambda b, j: (rev(b, j), 0)),
        out_shape=jax.ShapeDtypeStruct((t, vw), F32),
        scratch_shapes=[st_scratch],
        compiler_params=_cparams(2),
        name="retention_latent_backward",
    )(proj, proj, proj, s0_b, tabs["xib"], tabs["zb"], tabs["decb"])
    return pl.pallas_call(
        _ret_fwd_kernel,
        grid=(batch, nc),
        in_specs=[pl.BlockSpec((c, qw), lambda b, j: (fwd(b, j), 0)),
                  pl.BlockSpec((c, qw), lambda b, j: (fwd(b, j), 1)),
                  pl.BlockSpec((c, vw), lambda b, j: (fwd(b, j), 1)),
                  pl.BlockSpec((c, vw), lambda b, j: (fwd(b, j), 2)),
                  pl.BlockSpec((c, vw), lambda b, j: (fwd(b, j), 0)),
                  st_spec, tab_spec(tabs["amat"]), tab_spec(tabs["xif"]), tab_spec(tabs["zf"]),
                  tab_spec(tabs["decf"])],
        out_specs=pl.BlockSpec((c, vw), lambda b, j: (fwd(b, j), 0)),
        out_shape=jax.ShapeDtypeStruct((t, vw), BF16),
        scratch_shapes=[st_scratch],
        compiler_params=_cparams(2),
        name="retention_latent_forward",
    )(proj, proj, proj, proj, ob, s0_f, tabs["amat"], tabs["xif"], tabs["zf"], tabs["decf"])


def _attn_ctx_kernel(sink_ref, q_ref, k_ref, v_ref, o_ref):
    scale = HEAD_DIM ** -0.5
    outs = []
    for g in range(ATTN_KV):
        k_g = k_ref[:, g * HEAD_DIM:(g + 1) * HEAD_DIM].astype(BF16)
        v_g = v_ref[:, g * HEAD_DIM:(g + 1) * HEAD_DIM].astype(BF16)
        for r in range(ATTN_REP):
            hh = g * ATTN_REP + r
            q_h = q_ref[:, hh * HEAD_DIM:(hh + 1) * HEAD_DIM].astype(BF16)
            s = _dot_nt(q_h, k_g) * scale
            snk = sink_ref[hh]
            m = jnp.maximum(jnp.max(s, axis=-1, keepdims=True), snk)
            p = jnp.exp(s - m)
            den = jnp.sum(p, axis=-1, keepdims=True) + jnp.exp(snk - m)
            outs.append(_dot(p.astype(BF16), v_g) / den)
    o_ref[...] = jnp.concatenate(outs, axis=1).astype(o_ref.dtype)


def _attn_ctx(qkv, sink, batch, length):
    qw = ATTN_HEADS * HEAD_DIM
    kw = ATTN_KV * HEAD_DIM
    return pl.pallas_call(
        _attn_ctx_kernel,
        grid=(batch,),
        in_specs=[pl.BlockSpec(memory_space=pltpu.SMEM),
                  pl.BlockSpec((length, qw), lambda b: (b, 0)),
                  pl.BlockSpec((length, kw), lambda b: (b, qw // kw)),
                  pl.BlockSpec((length, kw), lambda b: (b, qw // kw + 1))],
        out_specs=pl.BlockSpec((length, qw), lambda b: (b, 0)),
        out_shape=jax.ShapeDtypeStruct((batch * length, qw), BF16),
        compiler_params=_cparams(1),
        name="attention_context",
    )(sink, qkv, qkv, qkv)


def _attn_lat_kernel(sink_ref, q_ref, k_ref, v_ref, kc_ref, vc_ref, o_ref, *, length):
    scale = HEAD_DIM ** -0.5
    band = 3 * QBLK
    j = pl.program_id(1)
    start = pl.multiple_of(jnp.clip((j - 1) * QBLK, 0, length - band), QBLK)
    rows = ATTN_REP * QBLK
    qpos = j * QBLK + lax.broadcasted_iota(I32, (rows, band), 0) % QBLK
    kpos = start + lax.broadcasted_iota(I32, (rows, band), 1)
    valid = jnp.abs(qpos - kpos) <= WINDOW
    rep_of_row = lax.broadcasted_iota(I32, (rows, 1), 0) // QBLK
    kb = k_ref[pl.ds(start, band), :]
    vb = v_ref[pl.ds(start, band), :]
    outs = []
    for g in range(ATTN_KV):
        cs = slice(g * HEAD_DIM, (g + 1) * HEAD_DIM)
        k_g = kb[:, cs].astype(BF16)
        v_g = vb[:, cs].astype(BF16)
        kc_g = kc_ref[:, cs].astype(BF16)
        vc_g = vc_ref[:, cs].astype(BF16)
        q_g = jnp.concatenate(
            [q_ref[:, (g * ATTN_REP + r) * HEAD_DIM:(g * ATTN_REP + r + 1) * HEAD_DIM] for r in range(ATTN_REP)],
            axis=0).astype(BF16)
        snk = jnp.zeros((rows, 1), F32)
        for r in range(ATTN_REP):
            snk = jnp.where(rep_of_row == r, sink_ref[g * ATTN_REP + r], snk)
        s_loc = jnp.where(valid, _dot_nt(q_g, k_g) * scale, NEG_INF)
        s_ctx = _dot_nt(q_g, kc_g) * scale
        m = jnp.maximum(jnp.maximum(jnp.max(s_loc, axis=-1, keepdims=True),
                                    jnp.max(s_ctx, axis=-1, keepdims=True)), snk)
        p_loc = jnp.exp(s_loc - m)
        p_ctx = jnp.exp(s_ctx - m)
        den = (jnp.sum(p_loc, axis=-1, keepdims=True) + jnp.sum(p_ctx, axis=-1, keepdims=True)) + jnp.exp(snk - m)
        o_g = (_dot(p_loc.astype(BF16), v_g) + _dot(p_ctx.astype(BF16), vc_g)) / den
        for r in range(ATTN_REP):
            outs.append(o_g[r * QBLK:(r + 1) * QBLK, :])
    o_ref[...] = jnp.concatenate(outs, axis=1).astype(o_ref.dtype)


def _attn_lat(qkv, sink, kc, vc, batch, length):
    qw = ATTN_HEADS * HEAD_DIM
    kw = ATTN_KV * HEAD_DIM
    nq = length // QBLK
    past = kc.shape[1]
    return pl.pallas_call(
        functools.partial(_attn_lat_kernel, length=length),
        grid=(batch, nq),
        in_specs=[pl.BlockSpec(memory_space=pltpu.SMEM),
                  pl.BlockSpec((QBLK, qw), lambda b, j: (b * nq + j, 0)),
                  pl.BlockSpec((length, kw), lambda b, j: (b, qw // kw)),
                  pl.BlockSpec((length, kw), lambda b, j: (b, qw // kw + 1)),
                  pl.BlockSpec((None, past, kw), lambda b, j: (b, 0, 0)),
                  pl.BlockSpec((None, past, kw), lambda b, j: (b, 0, 0))],
        out_specs=pl.BlockSpec((QBLK, qw), lambda b, j: (b * nq + j, 0)),
        out_shape=jax.ShapeDtypeStruct((batch * length, qw), BF16),
        compiler_params=_cparams(2),
        name="attention_latent",
    )(sink, qkv, qkv, qkv, kc, vc)


def _rope_tables(length):
    axis_dim = HEAD_DIM // 2
    pos = jnp.arange(length)
    row = (pos // GRID_W).astype(F32)
    colp = (pos % GRID_W).astype(F32)
    inv = ROPE_BASE ** (-jnp.arange(0, axis_dim, 2, dtype=F32) / axis_dim)
    ar = row[:, None] * inv[None, :]
    ac = colp[:, None] * inv[None, :]
    cos = jnp.concatenate([jnp.cos(ar), jnp.cos(ar), jnp.cos(ac), jnp.cos(ac)], axis=1)
    sin = jnp.concatenate([-jnp.sin(ar), jnp.sin(ar), -jnp.sin(ac), jnp.sin(ac)], axis=1)
    return jnp.concatenate([cos, cos], axis=1), jnp.concatenate([sin, sin], axis=1)


def _router_kernel(x_ref, g_ref, sh_ref, sc_ref, wr_ref, br_ref, h_ref, e_ref, p1_ref, p2_ref, n_ref, cnt_ref,
                   *, tiles_per_chunk):
    h = _norm_mod(x_ref[...], g_ref[...], sh_ref[...], sc_ref[...])
    _to_slabs(h_ref, h)
    logits = _dot3(h, wr_ref[...]) + br_ref[...]
    lane = lax.broadcasted_iota(I32, logits.shape, 1).astype(F32)
    ninf = -jnp.inf
    lg = jnp.where(lane < N_GROUPS, logits, ninf)
    gmax = jnp.max(lg, axis=-1, keepdims=True)
    gtop = jnp.min(jnp.where(lg == gmax, lane, float(LANE)), axis=-1, keepdims=True)
    pg_sel = 1.0 / jnp.sum(jnp.exp(lg - gmax), axis=-1, keepdims=True)
    lo = N_GROUPS + EXP_PER_GROUP * gtop
    le = jnp.where((lane >= lo) & (lane < lo + EXP_PER_GROUP), logits, ninf)
    v1 = jnp.max(le, axis=-1, keepdims=True)
    i1 = jnp.min(jnp.where(le == v1, lane, float(LANE)), axis=-1, keepdims=True)
    le2 = jnp.where(lane == i1, ninf, le)
    v2 = jnp.max(le2, axis=-1, keepdims=True)
    i2 = jnp.min(jnp.where(le2 == v2, lane, float(LANE)), axis=-1, keepdims=True)
    t = jnp.exp(v2 - v1)
    gate1 = pg_sel * (1.0 / (1.0 + t))
    gate2 = pg_sel * (t / (1.0 + t))
    p1_ref[...] = jnp.broadcast_to(gate1, p1_ref.shape)
    p2_ref[...] = jnp.broadcast_to(gate2, p2_ref.shape)

    e1 = i1 - N_GROUPS
    e2 = i2 - N_GROUPS
    tm = logits.shape[0]

    @pl.when(pl.program_id(0) % tiles_per_chunk == 0)
    def _():
        cnt_ref[...] = jnp.zeros_like(cnt_ref)

    carry = cnt_ref[0:1, :]
    oh1 = (lane == e1).astype(F32)
    oh2 = (lane == e2).astype(F32)
    tri = (lax.broadcasted_iota(I32, (tm, tm), 0) >= lax.broadcasted_iota(I32, (tm, tm), 1)).astype(BF16)
    cs1 = _dot(tri, oh1.astype(BF16))
    cs2 = _dot(tri, oh2.astype(BF16))
    tot1 = cs1[tm - 1:tm, :]
    tot2 = cs2[tm - 1:tm, :]
    rank1 = jnp.sum(oh1 * (carry + cs1 - 1.0), axis=-1, keepdims=True)
    rank2 = jnp.sum(oh2 * ((carry + tot1) + cs2 - 1.0), axis=-1, keepdims=True)
    code1 = rank1 * float(N_EXP) + e1
    code2 = rank2 * float(N_EXP) + e2
    e_ref[...] = jnp.where(lane == 0, code1, jnp.where(lane == 1, code2, 0.0)).astype(I32)
    new_cnt = (carry + tot1) + tot2
    cnt_ref[...] = jnp.broadcast_to(new_cnt, cnt_ref.shape)
    n_ref[...] = jnp.broadcast_to(new_cnt, n_ref.shape).astype(I32)


def _router(x, mods, layer, norm_g, wr, br, is_lat):
    t = x.shape[0]
    tm = 256
    seg_fn = _seg_of(is_lat, tm)
    tiles_per_chunk = SEG // tm
    return pl.pallas_call(
        functools.partial(_router_kernel, tiles_per_chunk=tiles_per_chunk),
        grid=(t // tm,),
        in_specs=[pl.BlockSpec((tm, D), lambda i: (i, 0)),
                  pl.BlockSpec((1, D), lambda i: (0, 0)),
                  _mod_spec(layer, 3, seg_fn),
                  _mod_spec(layer, 4, seg_fn),
                  pl.BlockSpec((D, LANE), lambda i: (0, 0)),
                  pl.BlockSpec((1, LANE), lambda i: (0, 0))],
        out_specs=[pl.BlockSpec((tm * SUB, LANE), lambda i: (i, 0)),
                   pl.BlockSpec((tm, LANE), lambda i: (i, 0)),
                   pl.BlockSpec((tm, LANE), lambda i: (i, 0)),
                   pl.BlockSpec((tm, LANE), lambda i: (i, 0)),
                   pl.BlockSpec((SUB, LANE), lambda i: (i // tiles_per_chunk, 0))],
        out_shape=[jax.ShapeDtypeStruct((t * SUB, LANE), F32),
                   jax.ShapeDtypeStruct((t, LANE), I32),
                   jax.ShapeDtypeStruct((t, LANE), F32),
                   jax.ShapeDtypeStruct((t, LANE), F32),
                   jax.ShapeDtypeStruct((t // SEG * SUB, LANE), I32)],
        scratch_shapes=[pltpu.VMEM((SUB, LANE), F32)],
        compiler_params=_cparams(1),
        name="moe_router",
    )(x, norm_g.reshape(1, D), mods, mods, wr, br)


def _moe_kernel(dest_ref, meta_ref, h_hbm, g1_hbm, g2_hbm, w1_ref, w3_ref, w2_ref, y_hbm,
                h_vm, y_vm, g_vm, xt, ot, gt, w1b, w3b, w2b, slot_tab, sem, *, tc, a_pad):
    chunk = pl.program_id(0)
    e = pl.program_id(1)
    stride = MOE_TILE_STRIDE
    n_lane_chunks = D // LANE
    h_rows = tc * SUB
    n_assign = 2 * tc
    slab_mask = (1 << 16) - 1

    def h_copy():
        return pltpu.make_async_copy(h_hbm.at[pl.ds(chunk * h_rows, h_rows)], h_vm.at[pl.ds(0, h_rows)], sem.at[0])

    def y_copy():
        return pltpu.make_async_copy(y_vm.at[pl.ds(0, h_rows)], y_hbm.at[pl.ds(chunk * h_rows, h_rows)], sem.at[1])

    def g_copy(k, g_hbm):
        return pltpu.make_async_copy(g_hbm.at[pl.ds(chunk * tc, tc)], g_vm.at[pl.ds(k * tc, tc)], sem.at[2 + k])

    @pl.when(e == 0)
    def _():
        h_copy().start()
        g_copy(0, g1_hbm).start()
        g_copy(1, g2_hbm).start()

        def zero(i, carry):
            y_vm[pl.ds(pl.multiple_of(i * 512, 512), 512), :] = jnp.zeros((512, LANE), F32)
            return carry

        lax.fori_loop(0, h_rows // 512, zero, 0)
        y_vm[pl.ds(h_rows, SUB), :] = jnp.zeros((SUB, LANE), F32)
        h_vm[pl.ds(h_rows, SUB), :] = jnp.zeros((SUB, LANE), F32)

        meta_base = chunk * 3 * N_EXP

        def fill_run(ex, carry):
            run_start = meta_ref[meta_base + ex]

            def fill(s, c2):
                slot_tab[run_start + s] = h_rows
                return c2

            return lax.fori_loop(meta_ref[meta_base + 2 * N_EXP + ex], meta_ref[meta_base + N_EXP + ex], fill, carry)

        lax.fori_loop(0, N_EXP, fill_run, 0)

        tok_step = SUB + (1 << 16)

        def invert(i, carry):
            entry0 = i * ((SUB // 2) * tok_step)
            code0 = chunk * n_assign + i * SUB
            for u in range(SUB):
                code = dest_ref[code0 + u]
                slot = meta_ref[meta_base + (code & (N_EXP - 1))] + lax.shift_right_logical(code, 5)
                slot_tab[slot] = entry0 + ((u // 2) * tok_step + (u % 2) * (tc << 16))
            return carry

        lax.fori_loop(0, n_assign // SUB, invert, 0)
        h_copy().wait()
        g_copy(0, g1_hbm).wait()
        g_copy(1, g2_hbm).wait()

    w1b[...] = w1_ref[...].astype(BF16)
    w3b[...] = w3_ref[...].astype(BF16)
    w2b[...] = w2_ref[...].astype(BF16)

    start = meta_ref[chunk * 3 * N_EXP + e]
    n_pad = meta_ref[chunk * 3 * N_EXP + N_EXP + e]

    def process(off, m):
        off = pl.multiple_of(off, MOE_PAD)

        def gather(i, carry):
            for u in range(SUB):
                r = i * SUB + u
                entry = slot_tab[off + r]
                t8 = pl.multiple_of(entry & slab_mask, SUB)
                xt[pl.ds(r, SUB, stride=stride), :] = h_vm[pl.ds(t8, SUB), :]
                gt[pl.ds(r, 1), :] = g_vm[pl.ds(lax.shift_right_logical(entry, 16), 1), :]
            return carry

        lax.fori_loop(0, m // SUB, gather, 0)
        x = jnp.concatenate([xt[pl.ds(c * stride, m), :] for c in range(n_lane_chunks)], axis=1).astype(BF16)
        a = _dot(x, w1b[...])
        b = _dot(x, w3b[...])
        hm = (_silu(a) * b).astype(BF16)
        out = _dot(hm, w2b[...])
        gate = gt[pl.ds(0, m), :]
        for c in range(n_lane_chunks):
            ot[pl.ds(c * stride, m), :] = out[:, c * LANE:(c + 1) * LANE] * gate

        def scatter(i, carry):
            for half in range(2):
                t8s, news = [], []
                for u in range(4):
                    r = i * SUB + half * 4 + u
                    t8 = pl.multiple_of(slot_tab[off + r] & slab_mask, SUB)
                    t8s.append(t8)
                    news.append(y_vm[pl.ds(t8, SUB), :] + ot[pl.ds(r, SUB, stride=stride), :])
                for u in range(4):
                    y_vm[pl.ds(t8s[u], SUB), :] = news[u]
            return carry

        lax.fori_loop(0, m // SUB, scatter, 0)

    n_full = n_pad // 256

    def full_body(i, carry):
        process(start + i * 256, 256)
        return carry

    lax.fori_loop(0, n_full, full_body, 0)
    rem = start + n_full * 256

    @pl.when((n_pad & 128) != 0)
    def _():
        process(rem, 128)

    @pl.when((n_pad & 64) != 0)
    def _():
        process(rem + (n_pad & 128), 64)

    @pl.when(e == N_EXP - 1)
    def _():
        y_copy().start()
        y_copy().wait()


def _moe(h_slabs, codes, counts, gate1, gate2, w1, w3, w2, layer):
    t = codes.shape[0]
    tc = SEG
    n_chunks = t // tc
    a_pad = 2 * tc + N_EXP * MOE_PAD
    padded = (counts + MOE_PAD - 1) // MOE_PAD * MOE_PAD
    meta = jnp.concatenate([jnp.cumsum(padded, axis=1) - padded, padded, counts], axis=1).astype(I32)
    dest = codes
    slab_rows = (tc + 1) * SUB
    tile_rows = (D // LANE) * MOE_TILE_STRIDE
    hbm_spec = pl.BlockSpec(memory_space=pl.ANY)
    grid_spec = pltpu.PrefetchScalarGridSpec(
        num_scalar_prefetch=2,
        grid=(n_chunks, N_EXP),
        in_specs=[hbm_spec, hbm_spec, hbm_spec,
                  pl.BlockSpec((None, None, D, D_FF), lambda c, e, *_: (layer, e, 0, 0)),
                  pl.BlockSpec((None, None, D, D_FF), lambda c, e, *_: (layer, e, 0, 0)),
                  pl.BlockSpec((None, None, D_FF, D), lambda c, e, *_: (layer, e, 0, 0))],
        out_specs=pl.BlockSpec(memory_space=pl.ANY),
        scratch_shapes=[pltpu.VMEM((slab_rows, LANE), F32),
                        pltpu.VMEM((slab_rows, LANE), F32),
                        pltpu.VMEM((2 * tc, LANE), F32),
                        pltpu.VMEM((tile_rows, LANE), F32),
                        pltpu.VMEM((tile_rows, LANE), F32),
                        pltpu.VMEM((256, LANE), F32),
                        pltpu.VMEM((D, D_FF), BF16),
                        pltpu.VMEM((D, D_FF), BF16),
                        pltpu.VMEM((D_FF, D), BF16),
                        pltpu.SMEM((a_pad,), I32),
                        pltpu.SemaphoreType.DMA((4,))],
    )
    return pl.pallas_call(
        functools.partial(_moe_kernel, tc=tc, a_pad=a_pad),
        grid_spec=grid_spec,
        out_shape=jax.ShapeDtypeStruct((t * SUB, LANE), F32),
        compiler_params=pltpu.CompilerParams(dimension_semantics=("arbitrary", "arbitrary"),
                                             vmem_limit_bytes=V7X_VMEM_LIMIT_MOE),
        name="moe_experts",
    )(dest.reshape(n_chunks * 2 * tc), meta.reshape(n_chunks * 3 * N_EXP), h_slabs, gate1, gate2, w1, w3, w2)


def kernel(x_prompt, x_sample, c, cache_attn_k, cache_attn_v, state_ret_fwd, state_ret_bwd, c_ctx, adaln_w, adaln_b, norm1_g, norm2_g, final_g, fnet_w, ret_w_in, ret_w_out, ret_decay_fwd, ret_decay_bwd, attn_w_in, attn_w_out, attn_sink, router_group_w, router_group_b, router_expert_w, router_expert_b, moe_w1, moe_w3, moe_w2):
    bp, lp, _ = x_prompt.shape
    bs, ls, _ = x_sample.shape
    xs = {False: x_prompt.reshape(bp * lp, D), True: x_sample.reshape(bs * ls, D)}
    dims = {False: (bp, lp), True: (bs, ls)}

    cond8 = jnp.concatenate([c_ctx[None, :], c, jnp.zeros((SUB - 1 - bs, D), F32)], axis=0)
    mods = _adaln(cond8, adaln_w, adaln_b).reshape(DEPTH, SUB, 6, 1, D)

    rope_lat = _rope_tables(ls)
    rope = {False: (jnp.ones((bp * lp, LANE), F32), jnp.zeros((bp * lp, LANE), F32)),
            True: (jnp.tile(rope_lat[0], (bs, 1)), jnp.tile(rope_lat[1], (bs, 1)))}

    extras = {}
    for i in range(DEPTH):
        kind, j = i % 3, i // 3
        if kind == 0:
            lw = dict(w_fold=_fnet_fold(fnet_w[j]))
        elif kind == 1:
            lw = dict(w_in=ret_w_in[j].astype(BF16), w_out=ret_w_out[j].astype(BF16),
                      tabs=_ret_tables(ret_decay_fwd[j], ret_decay_bwd[j]),
                      s0_f=state_ret_fwd[:, j], s0_b=state_ret_bwd[:, j])
        else:
            kvw = ATTN_KV * HEAD_DIM
            lw = dict(w_in=attn_w_in[j].astype(BF16), w_out=attn_w_out[j].astype(BF16), sink=attn_sink[j],
                      kc=cache_attn_k[:, j].reshape(bs, -1, kvw), vc=cache_attn_v[:, j].reshape(bs, -1, kvw))
        wr = jnp.concatenate([router_group_w[i], jnp.transpose(router_expert_w[i], (1, 0, 2)).reshape(D, N_EXP),
                              jnp.zeros((D, LANE - N_GROUPS - N_EXP), F32)], axis=1)
        br = jnp.concatenate([router_group_b[i], router_expert_b[i].reshape(N_EXP),
                              jnp.zeros((LANE - N_GROUPS - N_EXP,), F32)]).reshape(1, LANE)

        for is_lat in (False, True):
            b_, l_ = dims[is_lat]
            x = _mixer(i, xs[is_lat], mods, norm1_g[i], lw, is_lat, b_, l_, rope[is_lat], extras)
            xs[is_lat] = _ffn(i, x, mods, norm2_g[i], wr, br, moe_w1, moe_w3, moe_w2, is_lat)

    y_prompt = _final_norm(xs[False], final_g).reshape(bp, lp, D)
    y_sample = _final_norm(xs[True], final_g).reshape(bs, ls, D)
    return (y_prompt, y_sample, extras["k"], extras["v"], extras["sf"], extras["sb"])


def _mixer(i, x, mods, norm_g, lw, is_lat, b_, l_, rope, extras):
    kind = i % 3
    if kind == 0:
        y = _proj(x, mods, i, 0, norm_g, lw["w_fold"], is_lat, BF16)
        return _seqmix(y, x, mods, i, is_lat, b_, l_)
    if kind == 1:
        proj = _proj(x, mods, i, 0, norm_g, lw["w_in"], is_lat, BF16)
        if is_lat:
            yr = _ret_lat(proj, lw["tabs"], lw["s0_f"], lw["s0_b"], b_, l_)
        else:
            yr, sf, sb = _ret_ctx(proj, lw["tabs"], b_)
            extras["sf"], extras["sb"] = sf[:, None], sb[:, None]
        return _linres(yr, lw["w_out"], x, mods, i, 2, is_lat)
    qkv = _proj(x, mods, i, 0, norm_g, lw["w_in"], is_lat, F32, rope=rope)
    if is_lat:
        o = _attn_lat(qkv, lw["sink"], lw["kc"], lw["vc"], b_, l_)
    else:
        qw = ATTN_HEADS * HEAD_DIM
        kw = ATTN_KV * HEAD_DIM
        extras["k"] = qkv[:, qw:qw + kw].reshape(b_, 1, l_, ATTN_KV, HEAD_DIM)
        extras["v"] = qkv[:, qw + kw:].reshape(b_, 1, l_, ATTN_KV, HEAD_DIM)
        o = _attn_ctx(qkv, lw["sink"], b_, l_)
    return _linres(o, lw["w_out"], x, mods, i, 2, is_lat)


def _ffn(i, x, mods, norm_g, wr, br, w1, w3, w2, is_lat):
    h_slabs, codes, gate1, gate2, counts = _router(x, mods, i, norm_g, wr, br, is_lat)
    counts = counts.reshape(-1, SUB, LANE)[:, 0, :N_EXP]
    y_slabs = _moe(h_slabs, codes[:, :2], counts, gate1, gate2, w1, w3, w2, i)
    return _residual(x, y_slabs, mods, i, 5, is_lat)
```
